```python
import math
import jax, jax.numpy as jnp
from jax import lax
import numpy as np

D_MODEL = 1024
BATCH = 8
SEQ = 4096
DEPTH = 2
DEC_BATCH = 16
DEC_SEQ = 64
PAST_LEN = 2048

CHUNK = 64
MLP_CHUNK = 128
D_A = D_MODEL
N_GROUPS = 4
GROUP_W = D_A // N_GROUPS
N_HEADS = 8
HEAD_DIM = 64
V_DIM = 2 * HEAD_DIM
QK_W = N_HEADS * 2 * HEAD_DIM
D_B = N_HEADS * V_DIM
Q_BLOCK = 128
EPS = 1e-6
IN_SIZES = (D_A, D_A, D_A, QK_W, QK_W, D_B, D_B, D_MODEL, D_MODEL)
D_IN = D_A * 3 + QK_W * 2 + D_B * 2 + D_MODEL * 2

kernel_name = "gated_chunkmlp_diffattn_stream_step"


def rmsnorm(x, g):
    xf = x.astype(jnp.float32)
    y = xf * lax.rsqrt(jnp.mean(xf * xf, axis=-1, keepdims=True) + EPS)
    return (y * g.astype(jnp.float32)).astype(x.dtype)


def split_in(z):
    idx = []
    acc = 0
    for s in IN_SIZES[:-1]:
        acc += s
        idx.append(acc)
    return jnp.split(z, idx, axis=-1)


def masked_spatial(w_s):
    tril = jnp.tril(jnp.ones((MLP_CHUNK, MLP_CHUNK), dtype=bool))
    return jnp.where(tril[None], w_s, jnp.zeros_like(w_s))


def chunk_mlp_prompt(v, w_s, b_s):
    B, S, _ = v.shape
    n = S // MLP_CHUNK
    vr = v.reshape(B, n, MLP_CHUNK, N_GROUPS, GROUP_W)
    s = jnp.einsum('gts,bnsgc->bntgc', masked_spatial(w_s), vr)
    s = s + jnp.transpose(b_s)[None, None, :, :, None]
    return s.reshape(B, S, D_A)


def chunk_mlp_sample(v, w_s, b_s):
    B, L, _ = v.shape
    ws = masked_spatial(w_s)[:, :L, :L]
    vr = v.reshape(B, L, N_GROUPS, GROUP_W)
    s = jnp.einsum('gts,bsgc->btgc', ws, vr) + jnp.transpose(b_s[:, :L])[None, :, :, None]
    return s.reshape(B, L, D_A)


def diff_attn_core(q, k, v, lam, mask):
    scale = HEAD_DIM ** -0.5
    s = jnp.einsum('bqhjd,bkhjd->bhjqk', q, k).astype(jnp.float32) * scale
    if mask is not None:
        s = jnp.where(mask[None, None, None], s, jnp.float32(-1e30))
    p = jax.nn.softmax(s, axis=-1)
    a = p[:, :, 0] - lam * p[:, :, 1]
    return jnp.einsum('bhqk,bkhe->bqhe', a.astype(v.dtype), v)


def diff_attn_prompt(q, k, v, lam):
    B, S = q.shape[0], q.shape[1]
    nb = S // Q_BLOCK
    qb = jnp.moveaxis(q.reshape(B, nb, Q_BLOCK, N_HEADS, 2, HEAD_DIM), 1, 0)
    kpos = jnp.arange(S)

    def one_block(args):
        q_blk, blk = args
        qpos = blk * Q_BLOCK + jnp.arange(Q_BLOCK)
        mask = (kpos[None, :] // CHUNK) <= (qpos[:, None] // CHUNK)
        return diff_attn_core(q_blk, k, v, lam, mask)

    o = lax.map(one_block, (qb, jnp.arange(nb)))
    return jnp.moveaxis(o, 0, 1).reshape(B, S, N_HEADS, V_DIM)


def mixer_layer(x, li, norm_g, w_in, w_s, b_s, v_norm_g, lam_q1, lam_k1, lam_q2, lam_k2,
                attn_norm_g, w_pa, w_pb, w_out, cache_k=None, cache_v=None):
    B, L, _ = x.shape
    h = rmsnorm(x, norm_g)
    a_u, a_v, a_z, b_q, b_k, b_v, b_z, g_a, g_b = split_in(h @ w_in)
    a_u = jax.nn.gelu(a_u)
    a_v = rmsnorm(jax.nn.gelu(a_v), v_norm_g)
    if cache_k is None:
        spatial = chunk_mlp_prompt(a_v, w_s, b_s)
    else:
        spatial = chunk_mlp_sample(a_v, w_s, b_s)
    y_a = jax.nn.silu(a_z) * (a_u * spatial)
    q = b_q.reshape(B, L, N_HEADS, 2, HEAD_DIM)
    k = b_k.reshape(B, L, N_HEADS, 2, HEAD_DIM)
    v = b_v.reshape(B, L, N_HEADS, V_DIM)
    lam_init = 0.8 - 0.6 * math.exp(-0.3 * li)
    lam = (jnp.exp(jnp.sum(lam_q1.astype(jnp.float32) * lam_k1.astype(jnp.float32)))
           - jnp.exp(jnp.sum(lam_q2.astype(jnp.float32) * lam_k2.astype(jnp.float32)))
           + lam_init)
    if cache_k is None:
        o = diff_attn_prompt(q, k, v, lam)
    else:
        P = cache_k.shape[1]
        k_all = jnp.concatenate([cache_k.reshape(B, P, N_HEADS, 2, HEAD_DIM), k], axis=1)
        v_all = jnp.concatenate([cache_v, v], axis=1)
        o = diff_attn_core(q, k_all, v_all, lam, None)
    o = rmsnorm(o, attn_norm_g) * (1.0 - lam_init)
    y_b = jax.nn.silu(b_z) * o.reshape(B, L, D_B)
    merged = jax.nn.sigmoid(g_a) * (y_a @ w_pa) + jax.nn.sigmoid(g_b) * (y_b @ w_pb)
    x = x + merged @ w_out
    return x, k.reshape(B, L, N_HEADS, 2 * HEAD_DIM), v, a_v


def setup_inputs(seed: int = 0) -> dict:
    key = jax.random.key(seed)
    ks = jax.random.split(key, 20)
    nrm = jax.random.normal
    f32 = jnp.float32
    return {
        "x_prompt": nrm(ks[0], (BATCH, SEQ, D_MODEL), f32),
        "x_sample": nrm(ks[1], (DEC_BATCH, DEC_SEQ, D_MODEL), f32),
        "cache_k": nrm(ks[2], (DEPTH, DEC_BATCH, PAST_LEN, N_HEADS, 2 * HEAD_DIM), f32),
        "cache_v": nrm(ks[3], (DEPTH, DEC_BATCH, PAST_LEN, N_HEADS, V_DIM), f32),
        "norm_g": 1.0 + 0.01 * nrm(ks[4], (DEPTH, D_MODEL), f32),
        "w_in": nrm(ks[5], (DEPTH, D_MODEL, D_IN), f32) * D_MODEL ** -0.5,
        "w_s": nrm(ks[6], (DEPTH, N_GROUPS, MLP_CHUNK, MLP_CHUNK), f32) * (0.5 * MLP_CHUNK ** -0.5),
        "b_s": 1.0 + 0.01 * nrm(ks[7], (DEPTH, N_GROUPS, MLP_CHUNK), f32),
        "v_norm_g": 1.0 + 0.01 * nrm(ks[8], (DEPTH, D_A), f32),
        "lam_q1": 0.1 * nrm(ks[9], (DEPTH, HEAD_DIM), f32),
        "lam_k1": 0.1 * nrm(ks[10], (DEPTH, HEAD_DIM), f32),
        "lam_q2": 0.1 * nrm(ks[11], (DEPTH, HEAD_DIM), f32),
        "lam_k2": 0.1 * nrm(ks[12], (DEPTH, HEAD_DIM), f32),
        "attn_norm_g": 1.0 + 0.01 * nrm(ks[13], (DEPTH, V_DIM), f32),
        "w_pa": nrm(ks[14], (DEPTH, D_A, D_MODEL), f32) * D_A ** -0.5,
        "w_pb": nrm(ks[15], (DEPTH, D_B, D_MODEL), f32) * D_B ** -0.5,
        "w_out": nrm(ks[16], (DEPTH, D_MODEL, D_MODEL), f32) * D_MODEL ** -0.5,
        "final_norm_g": 1.0 + 0.01 * nrm(ks[17], (D_MODEL,), f32),
    }


def reference(x_prompt, x_sample, cache_k, cache_v, norm_g, w_in, w_s, b_s, v_norm_g,
              lam_q1, lam_k1, lam_q2, lam_k2, attn_norm_g, w_pa, w_pb, w_out, final_norm_g):
    xp, xs = x_prompt, x_sample
    kp_l, vp_l, ks_l, vs_l, as_l = [], [], [], [], []
    for li in range(DEPTH):
        w = (norm_g[li], w_in[li], w_s[li], b_s[li], v_norm_g[li], lam_q1[li], lam_k1[li],
             lam_q2[li], lam_k2[li], attn_norm_g[li], w_pa[li], w_pb[li], w_out[li])
        xp, kp, vp, _ = mixer_layer(xp, li, *w)
        xs, ks_, vs_, as_ = mixer_layer(xs, li, *w, cache_k=cache_k[li], cache_v=cache_v[li])
        kp_l.append(kp); vp_l.append(vp)
        ks_l.append(ks_); vs_l.append(vs_); as_l.append(as_)
    y_prompt = rmsnorm(xp, final_norm_g)
    y_sample = rmsnorm(xs, final_norm_g)
    new_k_prompt = jnp.stack(kp_l, axis=0)
    new_v_prompt = jnp.stack(vp_l, axis=0)
    new_k_sample = jnp.stack(ks_l, axis=0)
    new_v_sample = jnp.stack(vs_l, axis=0)
    new_mlpv_sample = jnp.stack(as_l, axis=0)
    return (y_prompt, y_sample, new_k_prompt, new_v_prompt, new_k_sample, new_v_sample, new_mlpv_sample)
```

```python
import functools
import math

import jax
import jax.numpy as jnp
from jax import lax
from jax.experimental import pallas as pl
from jax.experimental.pallas import tpu as pltpu

F32 = jnp.float32
BF16 = jnp.bfloat16

D_MODEL = 1024
N_GROUPS = 4
GROUP_W = D_MODEL // N_GROUPS
N_HEADS = 8
HEAD_DIM = 64
V_DIM = 2 * HEAD_DIM
CHUNK = 64
EPS = 1e-6
N_SPLITS = 9
NEG_BIG = -1e30
Q_SCALE = HEAD_DIM ** -0.5 * math.log2(math.e)
SQRT_2_OVER_PI = math.sqrt(2.0 / math.pi)

VMEM_LIMIT_BYTES = 56 * 1024 * 1024


def _gelu_tanh(x):
    return x * (0.5 * (1.0 + jnp.tanh(SQRT_2_OVER_PI * (x + 0.044715 * (x * x * x)))))


def _sigmoid(x):
    return 1.0 / (1.0 + jnp.exp(-x))


def _silu(x):
    return x * _sigmoid(x)


def _rms_scale(x):
    return x * lax.rsqrt(jnp.mean(x * x, axis=-1, keepdims=True) + EPS)


def _inproj_kernel(x_ref, ng_ref, win_ref, ws_ref, bs_ref, vg_ref, wpa_ref,
                   q_ref, k_ref, v_ref, zb_ref, ma_ref, gb_ref, *rest, t_chunk, emit_av):
    if emit_av:
        av_ref, vb_scr, sp_scr = rest
    else:
        vb_scr, sp_scr = rest
    tm = x_ref.shape[0]
    h = (_rms_scale(x_ref[...]) * ng_ref[...]).astype(BF16)

    def proj(j):
        return jnp.dot(h, win_ref[:, j * D_MODEL:(j + 1) * D_MODEL], preferred_element_type=F32)

    a_v = _rms_scale(_gelu_tanh(proj(1))) * vg_ref[...]
    if emit_av:
        av_ref[...] = a_v
    vb_scr[...] = a_v.astype(BF16)
    row = lax.broadcasted_iota(jnp.int32, (t_chunk, t_chunk), 0)
    col = lax.broadcasted_iota(jnp.int32, (t_chunk, t_chunk), 1)
    for g in range(N_GROUPS):
        w_g = jnp.where(col <= row, ws_ref[g], 0.0).astype(BF16)
        lanes = slice(g * GROUP_W, (g + 1) * GROUP_W)
        for c in range(tm // t_chunk):
            rows = slice(c * t_chunk, (c + 1) * t_chunk)
            sp_scr[rows, lanes] = (
                jnp.dot(w_g, vb_scr[rows, lanes], preferred_element_type=F32) + bs_ref[:, lanes])
    a_u = _gelu_tanh(proj(0))
    y_a = (_silu(proj(2)) * (a_u * sp_scr[...])).astype(BF16)
    p_a = jnp.dot(y_a, wpa_ref[...], preferred_element_type=F32)
    ma_ref[...] = (_sigmoid(proj(7)) * p_a).astype(BF16)
    q_ref[...] = (proj(3) * Q_SCALE).astype(BF16)
    k_ref[...] = proj(4)
    v_ref[...] = proj(5)
    zb_ref[...] = _silu(proj(6)).astype(BF16)
    gb_ref[...] = _sigmoid(proj(8)).astype(BF16)


def _inproj(x2d, norm_g, w_in_bf, w_s, bs_full, v_norm_g, w_pa_bf, *, t_chunk, tm, emit_av):
    m = x2d.shape[0]
    row_spec = pl.BlockSpec((tm, D_MODEL), lambda i: (i, 0))
    full = lambda shape: pl.BlockSpec(shape, lambda i: (0,) * len(shape))
    out_shape = [
        jax.ShapeDtypeStruct((m, D_MODEL), BF16),
        jax.ShapeDtypeStruct((m, D_MODEL), F32),
        jax.ShapeDtypeStruct((m, D_MODEL), F32),
        jax.ShapeDtypeStruct((m, D_MODEL), BF16),
        jax.ShapeDtypeStruct((m, D_MODEL), BF16),
        jax.ShapeDtypeStruct((m, D_MODEL), BF16),
    ]
    if emit_av:
        out_shape.append(jax.ShapeDtypeStruct((m, D_MODEL), F32))
    return pl.pallas_call(
        functools.partial(_inproj_kernel, t_chunk=t_chunk, emit_av=emit_av),
        grid=(m // tm,),
        in_specs=[
            row_spec,
            full((1, D_MODEL)),
            full((D_MODEL, N_SPLITS * D_MODEL)),
            full((N_GROUPS, t_chunk, t_chunk)),
            full((t_chunk, D_MODEL)),
            full((1, D_MODEL)),
            full((D_MODEL, D_MODEL)),
        ],
        out_specs=[row_spec] * len(out_shape),
        out_shape=out_shape,
        scratch_shapes=[pltpu.VMEM((tm, D_MODEL), BF16), pltpu.VMEM((tm, D_MODEL), F32)],
        compiler_params=pltpu.CompilerParams(
            dimension_semantics=("arbitrary",), vmem_limit_bytes=VMEM_LIMIT_BYTES),
        name="inproj",
    )(x2d, norm_g, w_in_bf, w_s, bs_full, v_norm_g, w_pa_bf)


def _lambda(lq1_ref, lk1_ref, lq2_ref, lk2_ref, lam_init):
    s1 = jnp.sum(lq1_ref[...] * lk1_ref[...], axis=-1, keepdims=True)
    s2 = jnp.sum(lq2_ref[...] * lk2_ref[...], axis=-1, keepdims=True)
    return jnp.exp(s1) - jnp.exp(s2) + lam_init


def _attn_prompt_kernel(lq1_ref, lk1_ref, lq2_ref, lk2_ref, ang_ref, q_ref, k_ref, v_ref, zb_ref,
                        o_ref, kb_scr, vt_scr, acc1_scr, acc2_scr, *, tq, lam_init):
    qi = pl.program_id(2)
    n_blocks = kb_scr.shape[0]

    @pl.when(qi == 0)
    def _():
        for c in range(n_blocks):
            rows = slice(c * tq, (c + 1) * tq)
            kb_scr[c] = k_ref[rows, :].astype(BF16)
            vt_scr[c] = v_ref[rows, :].T.astype(BF16)

    lam = _lambda(lq1_ref, lk1_ref, lq2_ref, lk2_ref, lam_init)
    q_t = q_ref[...].astype(F32).T
    comp = lax.broadcasted_iota(jnp.int32, q_t.shape, 0)
    qz = (jnp.where(comp < HEAD_DIM, q_t, 0.0).astype(BF16),
          jnp.where(comp >= HEAD_DIM, q_t, 0.0).astype(BF16))
    accs = (acc1_scr, acc2_scr)
    acc1_scr[...] = jnp.zeros_like(acc1_scr)
    acc2_scr[...] = jnp.zeros_like(acc2_scr)

    def step(j, stats, mask):
        k_blk = kb_scr[j]
        vt_blk = vt_scr[j]
        new_stats = []
        for a in range(2):
            m_old, l_old = stats[2 * a], stats[2 * a + 1]
            s_t = jnp.dot(k_blk, qz[a], preferred_element_type=F32)
            if mask is not None:
                s_t = jnp.where(mask, s_t, NEG_BIG)
            m_new = jnp.maximum(m_old, jnp.max(s_t, axis=0, keepdims=True))
            alpha = jnp.exp2(m_old - m_new)
            p_t = jnp.exp2(s_t - m_new)
            l_new = alpha * l_old + jnp.sum(p_t, axis=0, keepdims=True)
            accs[a][...] = alpha * accs[a][...] + jnp.dot(
                vt_blk, p_t.astype(BF16), preferred_element_type=F32)
            new_stats += [m_new, l_new]
        return tuple(new_stats)

    init = (jnp.full((1, tq), NEG_BIG, F32), jnp.zeros((1, tq), F32)) * 2
    stats = lax.fori_loop(0, qi, lambda j, st: step(j, st, None), init)
    k_chunk = lax.broadcasted_iota(jnp.int32, (tq, tq), 0) // CHUNK
    q_chunk = lax.broadcasted_iota(jnp.int32, (tq, tq), 1) // CHUNK
    m1, l1, m2, l2 = step(qi, stats, k_chunk <= q_chunk)

    o_t = acc1_scr[...] / l1 - lam * (acc2_scr[...] / l2)
    o_t = o_t * lax.rsqrt(jnp.mean(o_t * o_t, axis=0, keepdims=True) + EPS)
    o = o_t.T * (ang_ref[...] * (1.0 - lam_init))
    o_ref[...] = (zb_ref[...].astype(F32) * o).astype(BF16)


def _attn_prompt(lams, attn_norm_g, q, k, v, zb, *, lam_init, tq):
    b, s, _ = q.shape
    n_blocks = s // tq
    lam_spec = pl.BlockSpec((1, HEAD_DIM), lambda bi, hi, qi: (0, 0))
    head_rows = pl.BlockSpec((None, tq, V_DIM), lambda bi, hi, qi: (bi, qi, hi))
    head_all = pl.BlockSpec((None, s, V_DIM), lambda bi, hi, qi: (bi, 0, hi))
    return pl.pallas_call(
        functools.partial(_attn_prompt_kernel, tq=tq, lam_init=lam_init),
        grid=(b, N_HEADS, n_blocks),
        in_specs=[lam_spec] * 4 + [
            pl.BlockSpec((1, V_DIM), lambda bi, hi, qi: (0, 0)),
            head_rows, head_all, head_all, head_rows],
        out_specs=head_rows,
        out_shape=jax.ShapeDtypeStruct((b, s, D_MODEL), BF16),
        scratch_shapes=[
            pltpu.VMEM((n_blocks, tq, V_DIM), BF16),
            pltpu.VMEM((n_blocks, V_DIM, tq), BF16),
            pltpu.VMEM((V_DIM, tq), F32),
            pltpu.VMEM((V_DIM, tq), F32),
        ],
        compiler_params=pltpu.CompilerParams(
            dimension_semantics=("arbitrary", "arbitrary", "arbitrary"),
            vmem_limit_bytes=VMEM_LIMIT_BYTES),
        name="attn_prompt",
    )(*lams, attn_norm_g, q, k, v, zb)


def _attn_sample_kernel(lq1_ref, lk1_ref, lq2_ref, lk2_ref, ang_ref, q_ref, kn_ref, vn_ref,
                        kc_ref, vc_ref, zb_ref, o_ref, *, lam_init):
    lam = _lambda(lq1_ref, lk1_ref, lq2_ref, lk2_ref, lam_init)
    q = q_ref[...]
    comp = lax.broadcasted_iota(jnp.int32, q.shape, 1)
    zero = jnp.zeros_like(q)
    qz = (jnp.where(comp < HEAD_DIM, q, zero), jnp.where(comp >= HEAD_DIM, q, zero))
    k_c = kc_ref[...].astype(BF16)
    v_c = vc_ref[...].astype(BF16)
    k_n = kn_ref[...].astype(BF16)
    v_n = vn_ref[...].astype(BF16)
    contract_last = (((1,), (1,)), ((), ()))
    outs = []
    for a in range(2):
        s_c = lax.dot_general(qz[a], k_c, contract_last, preferred_element_type=F32)
        s_n = lax.dot_general(qz[a], k_n, contract_last, preferred_element_type=F32)
        m = jnp.maximum(jnp.max(s_c, axis=-1, keepdims=True), jnp.max(s_n, axis=-1, keepdims=True))
        p_c = jnp.exp2(s_c - m)
        p_n = jnp.exp2(s_n - m)
        l = jnp.sum(p_c, axis=-1, keepdims=True) + jnp.sum(p_n, axis=-1, keepdims=True)
        pv = (jnp.dot(p_c.astype(BF16), v_c, preferred_element_type=F32)
              + jnp.dot(p_n.astype(BF16), v_n, preferred_element_type=F32))
        outs.append(pv / l)
    o = outs[0] - lam * outs[1]
    o = _rms_scale(o) * (ang_ref[...] * (1.0 - lam_init))
    o_ref[...] = (zb_ref[...].astype(F32) * o).astype(BF16)


def _attn_sample(lams, attn_norm_g, q, k_new, v_new, cache_k, cache_v, zb, *, layer, lam_init):
    b, l, _ = q.shape
    past = cache_k.shape[2]
    lam_spec = pl.BlockSpec((1, HEAD_DIM), lambda bi, hi: (0, 0))
    head_new = pl.BlockSpec((None, l, V_DIM), lambda bi, hi: (bi, 0, hi))
    head_cache = pl.BlockSpec((None, None, past, V_DIM), lambda bi, hi: (layer, bi, 0, hi))
    return pl.pallas_call(
        functools.partial(_attn_sample_kernel, lam_init=lam_init),
        grid=(b, N_HEADS),
        in_specs=[lam_spec] * 4 + [
            pl.BlockSpec((1, V_DIM), lambda bi, hi: (0, 0)),
            head_new, head_new, head_new, head_cache, head_cache, head_new],
        out_specs=head_new,
        out_shape=jax.ShapeDtypeStruct((b, l, D_MODEL), BF16),
        compiler_params=pltpu.CompilerParams(
            dimension_semantics=("arbitrary", "arbitrary"), vmem_limit_bytes=VMEM_LIMIT_BYTES),
        name="attn_sample",
    )(*lams, attn_norm_g, q, k_new, v_new, cache_k, cache_v, zb)


def _outproj_kernel(x_ref, yb_ref, ma_ref, gb_ref, wpb_ref, wout_ref, fg_ref, o_ref, *, final):
    p_b = jnp.dot(yb_ref[...], wpb_ref[...], preferred_element_type=F32)
    merged = ma_ref[...].astype(F32) + gb_ref[...].astype(F32) * p_b
    x_new = x_ref[...] + jnp.dot(merged.astype(BF16), wout_ref[...], preferred_element_type=F32)
    if final:
        x_new = _rms_scale(x_new) * fg_ref[...]
    o_ref[...] = x_new


def _outproj(x2d, yb, ma, gb, w_pb_bf, w_out_bf, final_g, *, tm, final):
    m = x2d.shape[0]
    row_spec = pl.BlockSpec((tm, D_MODEL), lambda i: (i, 0))
    w_spec = pl.BlockSpec((D_MODEL, D_MODEL), lambda i: (0, 0))
    return pl.pallas_call(
        functools.partial(_outproj_kernel, final=final),
        grid=(m // tm,),
        in_specs=[row_spec, row_spec, row_spec, row_spec, w_spec, w_spec,
                  pl.BlockSpec((1, D_MODEL), lambda i: (0, 0))],
        out_specs=row_spec,
        out_shape=jax.ShapeDtypeStruct((m, D_MODEL), F32),
        compiler_params=pltpu.CompilerParams(
            dimension_semantics=("arbitrary",), vmem_limit_bytes=VMEM_LIMIT_BYTES),
        name="outproj",
    )(x2d, yb, ma, gb, w_pb_bf, w_out_bf, final_g)


def kernel(x_prompt, x_sample, cache_k, cache_v, norm_g, w_in, w_s, b_s, v_norm_g,
           lam_q1, lam_k1, lam_q2, lam_k2, attn_norm_g, w_pa, w_pb, w_out, final_norm_g):
    depth = w_in.shape[0]
    bp, sp, _ = x_prompt.shape
    bs_, ls, _ = x_sample.shape
    mlp_chunk = w_s.shape[-1]
    past = cache_k.shape[2]
    cache_k4 = cache_k.reshape(depth, bs_, past, D_MODEL)
    cache_v4 = cache_v.reshape(depth, bs_, past, D_MODEL)
    xp = x_prompt.reshape(bp * sp, D_MODEL)
    xs = x_sample.reshape(bs_ * ls, D_MODEL)
    final_g = final_norm_g.reshape(1, D_MODEL)
    kp_l, vp_l, ks_l, vs_l, av_l = [], [], [], [], []
    for li in range(depth):
        lam_init = 0.8 - 0.6 * math.exp(-0.3 * li)
        final = li == depth - 1
        w_in_bf = w_in[li].astype(BF16)
        w_pa_bf = w_pa[li].astype(BF16)
        w_pb_bf = w_pb[li].astype(BF16)
        w_out_bf = w_out[li].astype(BF16)
        ng = norm_g[li].reshape(1, D_MODEL)
        vg = v_norm_g[li].reshape(1, D_MODEL)
        ang = attn_norm_g[li].reshape(1, V_DIM)
        lams = tuple(a[li].reshape(1, HEAD_DIM) for a in (lam_q1, lam_k1, lam_q2, lam_k2))
        bs_full = jnp.repeat(jnp.transpose(b_s[li]), GROUP_W, axis=1)

        q, k, v, zb, ma, gb = _inproj(xp, ng, w_in_bf, w_s[li], bs_full, vg, w_pa_bf,
                                      t_chunk=mlp_chunk, tm=256, emit_av=False)
        yb = _attn_prompt(lams, ang, q.reshape(bp, sp, D_MODEL), k.reshape(bp, sp, D_MODEL),
                          v.reshape(bp, sp, D_MODEL), zb.reshape(bp, sp, D_MODEL),
                          lam_init=lam_init, tq=512)
        xp = _outproj(xp, yb.reshape(bp * sp, D_MODEL), ma, gb, w_pb_bf, w_out_bf, final_g,
                      tm=512, final=final)
        kp_l.append(k)
        vp_l.append(v)

        q, k, v, zb, ma, gb, av = _inproj(xs, ng, w_in_bf, w_s[li][:, :ls, :ls], bs_full[:ls],
                                          vg, w_pa_bf, t_chunk=ls, tm=256, emit_av=True)
        yb = _attn_sample(lams, ang, q.reshape(bs_, ls, D_MODEL), k.reshape(bs_, ls, D_MODEL),
                          v.reshape(bs_, ls, D_MODEL), cache_k4, cache_v4,
                          zb.reshape(bs_, ls, D_MODEL), layer=li, lam_init=lam_init)
        xs = _outproj(xs, yb.reshape(bs_ * ls, D_MODEL), ma, gb, w_pb_bf, w_out_bf, final_g,
                      tm=256, final=final)
        ks_l.append(k)
        vs_l.append(v)
        av_l.append(av)

    y_prompt = xp.reshape(bp, sp, D_MODEL)
    y_sample = xs.reshape(bs_, ls, D_MODEL)
    new_k_prompt = jnp.stack(kp_l).reshape(depth, bp, sp, N_HEADS, V_DIM)
    new_v_prompt = jnp.stack(vp_l).reshape(depth, bp, sp, N_HEADS, V_DIM)
    new_k_sample = jnp.stack(ks_l).reshape(depth, bs_, ls, N_HEADS, V_DIM)
    new_v_sample = jnp.stack(vs_l).reshape(depth, bs_, ls, N_HEADS, V_DIM)
    new_mlpv_sample = jnp.stack(av_l).reshape(depth, bs_, ls, D_MODEL)
    return (y_prompt, y_sample, new_k_prompt, new_v_prompt, new_k_sample, new_v_sample,
            new_mlpv_sample)
```

```python
import functools
import math

import jax
import jax.numpy as jnp
from jax import lax
from jax.experimental import pallas as pl
from jax.experimental.pallas import tpu as pltpu

F32 = jnp.float32
BF16 = jnp.bfloat16

D_MODEL = 1024
N_GROUPS = 4
GROUP_W = D_MODEL // N_GROUPS
N_HEADS = 8
HEAD_DIM = 64
V_DIM = 2 * HEAD_DIM
CHUNK = 64
EPS = 1e-6
N_SPLITS = 9
NEG_BIG = -1e30
Q_SCALE = HEAD_DIM ** -0.5 * math.log2(math.e)
SQRT_2_OVER_PI = math.sqrt(2.0 / math.pi)

VMEM_LIMIT_BYTES = 56 * 1024 * 1024


def _gelu_tanh(x):
    return x * (0.5 * (1.0 + jnp.tanh(SQRT_2_OVER_PI * (x + 0.044715 * (x * x * x)))))


def _sigmoid(x):
    return 1.0 / (1.0 + jnp.exp(-x))


def _silu(x):
    return x * _sigmoid(x)


def _rms_scale(x):
    return x * lax.rsqrt(jnp.mean(x * x, axis=-1, keepdims=True) + EPS)


def _store_heads(o_ref, val):
    tm = val.shape[0]
    for h in range(N_HEADS):
        o_ref[pl.ds(h, tm, stride=N_HEADS), :] = val[:, h * V_DIM:(h + 1) * V_DIM]


def _inproj_kernel(x_ref, ng_ref, win_ref, ws_ref, bs_ref, vg_ref, wpa_ref,
                   q_ref, kb_ref, vb_ref, k_ref, v_ref, zb_ref, ma_ref, gb_ref, *rest,
                   t_chunk, emit_av):
    if emit_av:
        av_ref, vb_scr, sp_scr = rest
    else:
        vb_scr, sp_scr = rest
    tm = x_ref.shape[0]
    h = (_rms_scale(x_ref[...]) * ng_ref[...]).astype(BF16)

    def proj(j):
        return jnp.dot(h, win_ref[:, j * D_MODEL:(j + 1) * D_MODEL], preferred_element_type=F32)

    a_v = _rms_scale(_gelu_tanh(proj(1))) * vg_ref[...]
    if emit_av:
        av_ref[...] = a_v
    vb_scr[...] = a_v.astype(BF16)
    row = lax.broadcasted_iota(jnp.int32, (t_chunk, t_chunk), 0)
    col = lax.broadcasted_iota(jnp.int32, (t_chunk, t_chunk), 1)
    for g in range(N_GROUPS):
        w_g = jnp.where(col <= row, ws_ref[g], 0.0).astype(BF16)
        lanes = slice(g * GROUP_W, (g + 1) * GROUP_W)
        for c in range(tm // t_chunk):
            rows = slice(c * t_chunk, (c + 1) * t_chunk)
            sp_scr[rows, lanes] = (
                jnp.dot(w_g, vb_scr[rows, lanes], preferred_element_type=F32) + bs_ref[:, lanes])
    a_u = _gelu_tanh(proj(0))
    y_a = (_silu(proj(2)) * (a_u * sp_scr[...])).astype(BF16)
    p_a = jnp.dot(y_a, wpa_ref[...], preferred_element_type=F32)
    ma_ref[...] = (_sigmoid(proj(7)) * p_a).astype(BF16)
    q_ref[...] = (proj(3) * Q_SCALE).astype(BF16)
    b_k = proj(4)
    kb_ref[...] = b_k.astype(BF16)
    _store_heads(k_ref, b_k)
    b_v = proj(5)
    vb_ref[...] = b_v.astype(BF16)
    _store_heads(v_ref, b_v)
    zb_ref[...] = _silu(proj(6)).astype(BF16)
    gb_ref[...] = _sigmoid(proj(8)).astype(BF16)


def _inproj(x2d, norm_g, w_in_bf, w_s, bs_full, v_norm_g, w_pa_bf, *, t_chunk, tm, emit_av):
    m = x2d.shape[0]
    row_spec = pl.BlockSpec((tm, D_MODEL), lambda i: (i, 0))
    head_spec = pl.BlockSpec((tm * N_HEADS, V_DIM), lambda i: (i, 0))
    full = lambda shape: pl.BlockSpec(shape, lambda i: (0,) * len(shape))
    out_shape = [
        jax.ShapeDtypeStruct((m, D_MODEL), BF16),
        jax.ShapeDtypeStruct((m, D_MODEL), BF16),
        jax.ShapeDtypeStruct((m, D_MODEL), BF16),
        jax.ShapeDtypeStruct((m * N_HEADS, V_DIM), F32),
        jax.ShapeDtypeStruct((m * N_HEADS, V_DIM), F32),
        jax.ShapeDtypeStruct((m, D_MODEL), BF16),
        jax.ShapeDtypeStruct((m, D_MODEL), BF16),
        jax.ShapeDtypeStruct((m, D_MODEL), BF16),
    ]
    out_specs = [row_spec, row_spec, row_spec, head_spec, head_spec, row_spec, row_spec, row_spec]
    if emit_av:
        out_shape.append(jax.ShapeDtypeStruct((m, D_MODEL), F32))
        out_specs.append(row_spec)
    return pl.pallas_call(
        functools.partial(_inproj_kernel, t_chunk=t_chunk, emit_av=emit_av),
        grid=(m // tm,),
        in_specs=[
            row_spec,
            full((1, D_MODEL)),
            full((D_MODEL, N_SPLITS * D_MODEL)),
            full((N_GROUPS, t_chunk, t_chunk)),
            full((t_chunk, D_MODEL)),
            full((1, D_MODEL)),
            full((D_MODEL, D_MODEL)),
        ],
        out_specs=out_specs,
        out_shape=out_shape,
        scratch_shapes=[pltpu.VMEM((tm, D_MODEL), BF16), pltpu.VMEM((tm, D_MODEL), F32)],
        compiler_params=pltpu.CompilerParams(
            dimension_semantics=("arbitrary",), vmem_limit_bytes=VMEM_LIMIT_BYTES),
        name="inproj",
    )(x2d, norm_g, w_in_bf, w_s, bs_full, v_norm_g, w_pa_bf)


def _lambda(lq1_ref, lk1_ref, lq2_ref, lk2_ref, lam_init):
    s1 = jnp.sum(lq1_ref[...] * lk1_ref[...], axis=-1, keepdims=True)
    s2 = jnp.sum(lq2_ref[...] * lk2_ref[...], axis=-1, keepdims=True)
    return jnp.exp(s1) - jnp.exp(s2) + lam_init


KEY_HALF = 256


def _attn_prompt_kernel(lq1_ref, lk1_ref, lq2_ref, lk2_ref, ang_ref, q_ref, k_ref, v_ref, zb_ref,
                        o_ref, kb_scr, vt_scr, qz_scr, s_scr, acc_scr, *, tq, lam_init):
    qi = pl.program_id(2)
    n_halves = kb_scr.shape[0]
    assert tq == 2 * KEY_HALF

    @pl.when(qi == 0)
    def _():
        for c in range(n_halves):
            rows = slice(c * KEY_HALF, (c + 1) * KEY_HALF)
            kb_scr[c] = k_ref[rows, :]
            vt_scr[c] = v_ref[rows, :].astype(F32).T.astype(BF16)

    lam = _lambda(lq1_ref, lk1_ref, lq2_ref, lk2_ref, lam_init)
    q_t = q_ref[...].astype(F32).T
    comp = lax.broadcasted_iota(jnp.int32, q_t.shape, 0)
    qz_scr[0] = jnp.where(comp < HEAD_DIM, q_t, 0.0).astype(BF16)
    qz_scr[1] = jnp.where(comp >= HEAD_DIM, q_t, 0.0).astype(BF16)
    acc_scr[...] = jnp.zeros_like(acc_scr)

    def scores(j, a):
        return jnp.dot(kb_scr[j], qz_scr[a], preferred_element_type=F32)

    def softmax_pv(s_t, j, a, m_old, l_old, lanes=slice(None)):
        m_new = jnp.maximum(m_old, jnp.max(s_t, axis=0, keepdims=True))
        alpha = jnp.exp2(m_old - m_new)
        p_t = jnp.exp2(s_t - m_new)
        l_new = alpha * l_old + jnp.sum(p_t, axis=0, keepdims=True)
        acc_scr[a, :, lanes] = alpha * acc_scr[a, :, lanes] + jnp.dot(
            vt_scr[j], p_t.astype(BF16), preferred_element_type=F32)
        return m_new, l_new

    def pair(t, stats):
        out = []
        for a in range(2):
            m, l = stats[2 * a], stats[2 * a + 1]
            s_a = s_scr[a]
            s_b = scores(2 * t + 1, a)
            m, l = softmax_pv(s_a, 2 * t, a, m, l)
            s_scr[a] = scores(2 * t + 2, a)
            m, l = softmax_pv(s_b, 2 * t + 1, a, m, l)
            out += [m, l]
        return tuple(out)

    for a in range(2):
        s_scr[a] = scores(0, a)
    init = (jnp.full((1, tq), NEG_BIG, F32), jnp.zeros((1, tq), F32)) * 2
    stats = lax.fori_loop(0, qi, pair, init)
    k_chunk = lax.broadcasted_iota(jnp.int32, (KEY_HALF, tq), 0) // CHUNK
    q_chunk = lax.broadcasted_iota(jnp.int32, (KEY_HALF, tq), 1) // CHUNK
    visible = k_chunk <= q_chunk
    late = slice(KEY_HALF, tq)
    l_fin = []
    for a in range(2):
        m, l = stats[2 * a], stats[2 * a + 1]
        s_b = jnp.dot(kb_scr[2 * qi + 1], qz_scr[a, :, late], preferred_element_type=F32)
        m, l = softmax_pv(jnp.where(visible, s_scr[a], NEG_BIG), 2 * qi, a, m, l)
        _, l_late = softmax_pv(jnp.where(visible[:, :KEY_HALF], s_b, NEG_BIG), 2 * qi + 1, a,
                               m[:, late], l[:, late], late)
        l_fin.append(jnp.concatenate([l[:, :KEY_HALF], l_late], axis=1))
    l1, l2 = l_fin

    o_t = acc_scr[0] / l1 - lam * (acc_scr[1] / l2)
    o_t = o_t * lax.rsqrt(jnp.mean(o_t * o_t, axis=0, keepdims=True) + EPS)
    o = o_t.T * (ang_ref[...] * (1.0 - lam_init))
    o_ref[...] = (zb_ref[...].astype(F32) * o).astype(BF16)


def _attn_prompt(lams, attn_norm_g, q, k, v, zb, *, lam_init, tq):
    b, s, _ = q.shape
    n_halves = s // KEY_HALF
    lam_spec = pl.BlockSpec((1, HEAD_DIM), lambda bi, hi, qi: (0, 0))
    head_rows = pl.BlockSpec((None, tq, V_DIM), lambda bi, hi, qi: (bi, qi, hi))
    head_all = pl.BlockSpec((None, s, V_DIM), lambda bi, hi, qi: (bi, 0, hi))
    return pl.pallas_call(
        functools.partial(_attn_prompt_kernel, tq=tq, lam_init=lam_init),
        grid=(b, N_HEADS, s // tq),
        in_specs=[lam_spec] * 4 + [
            pl.BlockSpec((1, V_DIM), lambda bi, hi, qi: (0, 0)),
            head_rows, head_all, head_all, head_rows],
        out_specs=head_rows,
        out_shape=jax.ShapeDtypeStruct((b, s, D_MODEL), BF16),
        scratch_shapes=[
            pltpu.VMEM((n_halves, KEY_HALF, V_DIM), BF16),
            pltpu.VMEM((n_halves, V_DIM, KEY_HALF), BF16),
            pltpu.VMEM((2, V_DIM, tq), BF16),
            pltpu.VMEM((2, KEY_HALF, tq), F32),
            pltpu.VMEM((2, V_DIM, tq), F32),
        ],
        compiler_params=pltpu.CompilerParams(
            dimension_semantics=("arbitrary", "arbitrary", "arbitrary"),
            vmem_limit_bytes=VMEM_LIMIT_BYTES),
        name="attn_prompt",
    )(*lams, attn_norm_g, q, k, v, zb)


def _attn_sample_kernel(lq1_ref, lk1_ref, lq2_ref, lk2_ref, ang_ref, q_ref, kn_ref, vn_ref,
                        kc_ref, vc_ref, zb_ref, o_ref, *, lam_init):
    lam = _lambda(lq1_ref, lk1_ref, lq2_ref, lk2_ref, lam_init)
    q = q_ref[...]
    comp = lax.broadcasted_iota(jnp.int32, q.shape, 1)
    zero = jnp.zeros_like(q)
    qz = (jnp.where(comp < HEAD_DIM, q, zero), jnp.where(comp >= HEAD_DIM, q, zero))
    k_c = kc_ref[...].astype(BF16)
    v_c = vc_ref[...].astype(BF16)
    k_n = kn_ref[...]
    v_n = vn_ref[...]
    contract_last = (((1,), (1,)), ((), ()))
    outs = []
    for a in range(2):
        s_c = lax.dot_general(qz[a], k_c, contract_last, preferred_element_type=F32)
        s_n = lax.dot_general(qz[a], k_n, contract_last, preferred_element_type=F32)
        m = jnp.maximum(jnp.max(s_c, axis=-1, keepdims=True), jnp.max(s_n, axis=-1, keepdims=True))
        p_c = jnp.exp2(s_c - m)
        p_n = jnp.exp2(s_n - m)
        l = jnp.sum(p_c, axis=-1, keepdims=True) + jnp.sum(p_n, axis=-1, keepdims=True)
        pv = (jnp.dot(p_c.astype(BF16), v_c, preferred_element_type=F32)
              + jnp.dot(p_n.astype(BF16), v_n, preferred_element_type=F32))
        outs.append(pv / l)
    o = outs[0] - lam * outs[1]
    o = _rms_scale(o) * (ang_ref[...] * (1.0 - lam_init))
    o_ref[...] = (zb_ref[...].astype(F32) * o).astype(BF16)


def _attn_sample(lams, attn_norm_g, q, k_new, v_new, cache_k, cache_v, zb, *, layer, lam_init):
    b, l, _ = q.shape
    past = cache_k.shape[2]
    lam_spec = pl.BlockSpec((1, HEAD_DIM), lambda bi, hi: (0, 0))
    head_new = pl.BlockSpec((None, l, V_DIM), lambda bi, hi: (bi, 0, hi))
    head_cache = pl.BlockSpec((None, None, past, V_DIM), lambda bi, hi: (layer, bi, 0, hi))
    return pl.pallas_call(
        functools.partial(_attn_sample_kernel, lam_init=lam_init),
        grid=(b, N_HEADS),
        in_specs=[lam_spec] * 4 + [
            pl.BlockSpec((1, V_DIM), lambda bi, hi: (0, 0)),
            head_new, head_new, head_new, head_cache, head_cache, head_new],
        out_specs=head_new,
        out_shape=jax.ShapeDtypeStruct((b, l, D_MODEL), BF16),
        compiler_params=pltpu.CompilerParams(
            dimension_semantics=("arbitrary", "arbitrary"), vmem_limit_bytes=VMEM_LIMIT_BYTES),
        name="attn_sample",
    )(*lams, attn_norm_g, q, k_new, v_new, cache_k, cache_v, zb)


def _outproj_kernel(x_ref, yb_ref, ma_ref, gb_ref, wpb_ref, wout_ref, fg_ref, o_ref, *, final):
    p_b = jnp.dot(yb_ref[...], wpb_ref[...], preferred_element_type=F32)
    merged = ma_ref[...].astype(F32) + gb_ref[...].astype(F32) * p_b
    x_new = x_ref[...] + jnp.dot(merged.astype(BF16), wout_ref[...], preferred_element_type=F32)
    if final:
        x_new = _rms_scale(x_new) * fg_ref[...]
    o_ref[...] = x_new


def _outproj(x2d, yb, ma, gb, w_pb_bf, w_out_bf, final_g, *, tm, final):
    m = x2d.shape[0]
    row_spec = pl.BlockSpec((tm, D_MODEL), lambda i: (i, 0))
    w_spec = pl.BlockSpec((D_MODEL, D_MODEL), lambda i: (0, 0))
    return pl.pallas_call(
        functools.partial(_outproj_kernel, final=final),
        grid=(m // tm,),
        in_specs=[row_spec, row_spec, row_spec, row_spec, w_spec, w_spec,
                  pl.BlockSpec((1, D_MODEL), lambda i: (0, 0))],
        out_specs=row_spec,
        out_shape=jax.ShapeDtypeStruct((m, D_MODEL), F32),
        compiler_params=pltpu.CompilerParams(
            dimension_semantics=("arbitrary",), vmem_limit_bytes=VMEM_LIMIT_BYTES),
        name="outproj",
    )(x2d, yb, ma, gb, w_pb_bf, w_out_bf, final_g)


def kernel(x_prompt, x_sample, cache_k, cache_v, norm_g, w_in, w_s, b_s, v_norm_g,
           lam_q1, lam_k1, lam_q2, lam_k2, attn_norm_g, w_pa, w_pb, w_out, final_norm_g):
    depth = w_in.shape[0]
    bp, sp, _ = x_prompt.shape
    bs_, ls, _ = x_sample.shape
    mlp_chunk = w_s.shape[-1]
    past = cache_k.shape[2]
    cache_k4 = cache_k.reshape(depth, bs_, past, D_MODEL)
    cache_v4 = cache_v.reshape(depth, bs_, past, D_MODEL)
    xp = x_prompt.reshape(bp * sp, D_MODEL)
    xs = x_sample.reshape(bs_ * ls, D_MODEL)
    final_g = final_norm_g.reshape(1, D_MODEL)
    kp_l, vp_l, ks_l, vs_l, av_l = [], [], [], [], []
    for li in range(depth):
        lam_init = 0.8 - 0.6 * math.exp(-0.3 * li)
        final = li == depth - 1
        w_in_bf = w_in[li].astype(BF16)
        w_pa_bf = w_pa[li].astype(BF16)
        w_pb_bf = w_pb[li].astype(BF16)
        w_out_bf = w_out[li].astype(BF16)
        ng = norm_g[li].reshape(1, D_MODEL)
        vg = v_norm_g[li].reshape(1, D_MODEL)
        ang = attn_norm_g[li].reshape(1, V_DIM)
        lams = tuple(a[li].reshape(1, HEAD_DIM) for a in (lam_q1, lam_k1, lam_q2, lam_k2))
        bs_full = jnp.repeat(jnp.transpose(b_s[li]), GROUP_W, axis=1)

        q, kb, vb, k, v, zb, ma, gb = _inproj(xp, ng, w_in_bf, w_s[li], bs_full, vg, w_pa_bf,
                                              t_chunk=mlp_chunk, tm=256, emit_av=False)
        to_seq = lambda a: a.reshape(bp, sp, D_MODEL)
        yb = _attn_prompt(lams, ang, to_seq(q), to_seq(kb), to_seq(vb), to_seq(zb),
                          lam_init=lam_init, tq=512)
        xp = _outproj(xp, yb.reshape(bp * sp, D_MODEL), ma, gb, w_pb_bf, w_out_bf, final_g,
                      tm=512, final=final)
        kp_l.append(k.reshape(bp, sp, N_HEADS, V_DIM))
        vp_l.append(v.reshape(bp, sp, N_HEADS, V_DIM))

        q, kb, vb, k, v, zb, ma, gb, av = _inproj(
            xs, ng, w_in_bf, w_s[li][:, :ls, :ls], bs_full[:ls], vg, w_pa_bf,
            t_chunk=ls, tm=256, emit_av=True)
        to_seq = lambda a: a.reshape(bs_, ls, D_MODEL)
        yb = _attn_sample(lams, ang, to_seq(q), to_seq(kb), to_seq(vb), cache_k4, cache_v4,
                          to_seq(zb), layer=li, lam_init=lam_init)
        xs = _outproj(xs, yb.reshape(bs_ * ls, D_MODEL), ma, gb, w_pb_bf, w_out_bf, final_g,
                      tm=256, final=final)
        ks_l.append(k.reshape(bs_, ls, N_HEADS, V_DIM))
        vs_l.append(v.reshape(bs_, ls, N_HEADS, V_DIM))
        av_l.append(av)

    y_prompt = xp.reshape(bp, sp, D_MODEL)
    y_sample = xs.reshape(bs_, ls, D_MODEL)
    new_k_prompt = jnp.stack(kp_l)
    new_v_prompt = jnp.stack(vp_l)
    new_k_sample = jnp.stack(ks_l)
    new_v_sample = jnp.stack(vs_l)
    new_mlpv_sample = jnp.stack(av_l).reshape(depth, bs_, ls, D_MODEL)
    return (y_prompt, y_sample, new_k_prompt, new_v_prompt, new_k_sample, new_v_sample,
            new_mlpv_sample)
```

```python
import functools
import math

import jax
import jax.numpy as jnp
from jax import lax
from jax.experimental import pallas as pl
from jax.experimental.pallas import tpu as pltpu

F32 = jnp.float32
BF16 = jnp.bfloat16

D_MODEL = 1024
N_GROUPS = 4
GROUP_W = D_MODEL // N_GROUPS
N_HEADS = 8
HEAD_DIM = 64
V_DIM = 2 * HEAD_DIM
CHUNK = 64
EPS = 1e-6
N_SPLITS = 9
NEG_BIG = -1e30
Q_SCALE = HEAD_DIM ** -0.5 * math.log2(math.e)
SQRT_2_OVER_PI = math.sqrt(2.0 / math.pi)

VMEM_LIMIT_BYTES = 56 * 1024 * 1024


def _gelu_tanh(x):
    return x * (0.5 * (1.0 + jnp.tanh(SQRT_2_OVER_PI * (x + 0.044715 * (x * x * x)))))


def _sigmoid(x):
    return 1.0 / (1.0 + jnp.exp(-x))


def _silu(x):
    return x * _sigmoid(x)


def _rms_scale(x):
    return x * lax.rsqrt(jnp.mean(x * x, axis=-1, keepdims=True) + EPS)


def _store_heads(o_ref, val):
    tm = val.shape[0]
    for h in range(N_HEADS):
        o_ref[pl.ds(h, tm, stride=N_HEADS), :] = val[:, h * V_DIM:(h + 1) * V_DIM]


def _inproj_kernel(x_ref, ng_ref, win_ref, ws_ref, bs_ref, vg_ref, wpa_ref, *rest,
                   t_chunk, emit_av, layer):
    n_hist = 3 if emit_av else 2
    prev = rest[:n_hist] if layer else ()
    rest = rest[len(prev):]
    q_ref, kb_ref, vb_ref, zb_ref, ma_ref, gb_ref = rest[:6]
    hist = rest[6:6 + n_hist]
    vb_scr, sp_scr = rest[6 + n_hist:]
    for old, new in zip(prev, hist):
        new[:layer] = old[...]
    k_ref, v_ref = hist[0].at[layer], hist[1].at[layer]
    if emit_av:
        av_ref = hist[2].at[layer]
    tm = x_ref.shape[0]
    h = (_rms_scale(x_ref[...]) * ng_ref[...]).astype(BF16)

    def proj(j):
        return jnp.dot(h, win_ref[:, j * D_MODEL:(j + 1) * D_MODEL], preferred_element_type=F32)

    a_v = _rms_scale(_gelu_tanh(proj(1))) * vg_ref[...]
    if emit_av:
        av_ref[...] = a_v
    vb_scr[...] = a_v.astype(BF16)
    row = lax.broadcasted_iota(jnp.int32, (t_chunk, t_chunk), 0)
    col = lax.broadcasted_iota(jnp.int32, (t_chunk, t_chunk), 1)
    for g in range(N_GROUPS):
        w_g = jnp.where(col <= row, ws_ref[g], 0.0).astype(BF16)
        lanes = slice(g * GROUP_W, (g + 1) * GROUP_W)
        for c in range(tm // t_chunk):
            rows = slice(c * t_chunk, (c + 1) * t_chunk)
            sp_scr[rows, lanes] = (
                jnp.dot(w_g, vb_scr[rows, lanes], preferred_element_type=F32) + bs_ref[:, lanes])
    a_u = _gelu_tanh(proj(0))
    y_a = (_silu(proj(2)) * (a_u * sp_scr[...])).astype(BF16)
    p_a = jnp.dot(y_a, wpa_ref[...], preferred_element_type=F32)
    ma_ref[...] = (_sigmoid(proj(7)) * p_a).astype(BF16)
    q_ref[...] = (proj(3) * Q_SCALE).astype(BF16)
    b_k = proj(4)
    kb_ref[...] = b_k.astype(BF16)
    _store_heads(k_ref, b_k)
    b_v = proj(5)
    vb_ref[...] = b_v.astype(BF16)
    _store_heads(v_ref, b_v)
    zb_ref[...] = _silu(proj(6)).astype(BF16)
    gb_ref[...] = _sigmoid(proj(8)).astype(BF16)


def _inproj(x2d, norm_g, w_in_bf, w_s, bs_full, v_norm_g, w_pa_bf, prev, *, t_chunk, tm, emit_av,
            layer):
    m = x2d.shape[0]
    row_spec = pl.BlockSpec((tm, D_MODEL), lambda i: (i, 0))
    full = lambda shape: pl.BlockSpec(shape, lambda i: (0,) * len(shape))
    hist_tiles = [(tm * N_HEADS, V_DIM), (tm * N_HEADS, V_DIM)] + ([(tm, D_MODEL)] if emit_av else [])
    hist_rows = [m * N_HEADS, m * N_HEADS] + ([m] if emit_av else [])
    hist_spec = lambda n, tile: pl.BlockSpec((n,) + tile, lambda i: (0, i, 0))
    bf16_out = jax.ShapeDtypeStruct((m, D_MODEL), BF16)
    out_shape = [bf16_out] * 6 + [
        jax.ShapeDtypeStruct((layer + 1, rows, tile[1]), F32)
        for rows, tile in zip(hist_rows, hist_tiles)]
    out_specs = [row_spec] * 6 + [hist_spec(layer + 1, tile) for tile in hist_tiles]
    prev_specs = [hist_spec(layer, tile) for tile in hist_tiles] if layer else []
    return pl.pallas_call(
        functools.partial(_inproj_kernel, t_chunk=t_chunk, emit_av=emit_av, layer=layer),
        grid=(m // tm,),
        in_specs=[
            row_spec,
            full((1, D_MODEL)),
            full((D_MODEL, N_SPLITS * D_MODEL)),
            full((N_GROUPS, t_chunk, t_chunk)),
            full((t_chunk, D_MODEL)),
            full((1, D_MODEL)),
            full((D_MODEL, D_MODEL)),
        ] + prev_specs,
        out_specs=out_specs,
        out_shape=out_shape,
        scratch_shapes=[pltpu.VMEM((tm, D_MODEL), BF16), pltpu.VMEM((tm, D_MODEL), F32)],
        compiler_params=pltpu.CompilerParams(
            dimension_semantics=("arbitrary",), vmem_limit_bytes=VMEM_LIMIT_BYTES),
        name="inproj",
    )(x2d, norm_g, w_in_bf, w_s, bs_full, v_norm_g, w_pa_bf, *prev)


def _lambda(lq1_ref, lk1_ref, lq2_ref, lk2_ref, lam_init):
    s1 = jnp.sum(lq1_ref[...] * lk1_ref[...], axis=-1, keepdims=True)
    s2 = jnp.sum(lq2_ref[...] * lk2_ref[...], axis=-1, keepdims=True)
    return jnp.exp(s1) - jnp.exp(s2) + lam_init


KEY_HALF = 256
LANE_TILE = 128
HEADS_PER_STEP = 2
VT_ROWS = V_DIM + 16


def _attn_prompt_kernel(lq1_ref, lk1_ref, lq2_ref, lk2_ref, ang_ref, q_ref, k_ref, v_ref, zb_ref,
                        o_ref, kb_scr, vt_scr, qz_scr, s_scr, acc_scr, *, tq, lam_init):
    qi = pl.program_id(2)
    n_halves = kb_scr.shape[1]
    assert tq == 2 * KEY_HALF
    chains = [(hh, a) for hh in range(HEADS_PER_STEP) for a in range(2)]

    @pl.when(qi == 0)
    def _():
        tail = lax.broadcasted_iota(jnp.int32, (VT_ROWS - V_DIM, KEY_HALF), 0)
        ones_row = jnp.where(tail == 0, 1.0, 0.0).astype(BF16)
        for hh in range(HEADS_PER_STEP):
            lanes = slice(hh * V_DIM, (hh + 1) * V_DIM)
            for c in range(n_halves):
                rows = slice(c * KEY_HALF, (c + 1) * KEY_HALF)
                kb_scr[hh, c] = k_ref[rows, lanes]
                vt_scr[hh, c, :V_DIM, :] = v_ref[rows, lanes].astype(F32).T.astype(BF16)
                vt_scr[hh, c, V_DIM:, :] = ones_row

    lam = _lambda(lq1_ref, lk1_ref, lq2_ref, lk2_ref, lam_init)
    comp = lax.broadcasted_iota(jnp.int32, (V_DIM, tq), 0)
    for hh in range(HEADS_PER_STEP):
        q_t = q_ref[:, hh * V_DIM:(hh + 1) * V_DIM].astype(F32).T
        qz_scr[hh, 0] = jnp.where(comp < HEAD_DIM, q_t, 0.0).astype(BF16)
        qz_scr[hh, 1] = jnp.where(comp >= HEAD_DIM, q_t, 0.0).astype(BF16)
    acc_scr[...] = jnp.zeros_like(acc_scr)

    def scores(hh, a, j, lanes=slice(None)):
        return jnp.dot(kb_scr[hh, j], qz_scr[hh, a, :, lanes], preferred_element_type=F32)

    def softmax_pv(s_t, hh, a, j, m_old, lanes=slice(None), mask=None):
        m_new, alpha, p_t = [], [], []
        for c in range(m_old.shape[1] // LANE_TILE):
            strip = slice(c * LANE_TILE, (c + 1) * LANE_TILE)
            s_c = s_t[:, strip]
            if mask is not None:
                s_c = jnp.where(mask[:, strip], s_c, NEG_BIG)
            m_c = jnp.maximum(m_old[:, strip], jnp.max(s_c, axis=0, keepdims=True))
            alpha.append(jnp.exp2(m_old[:, strip] - m_c))
            p_t.append(jnp.exp2(s_c - m_c).astype(BF16))
            m_new.append(m_c)
        acc_scr[hh, a, :, lanes] = jnp.concatenate(alpha, axis=1) * acc_scr[hh, a, :, lanes] + (
            jnp.dot(vt_scr[hh, j], jnp.concatenate(p_t, axis=1), preferred_element_type=F32))
        return jnp.concatenate(m_new, axis=1)

    def pair(t, ms):
        out = []
        for (hh, a), m in zip(chains, ms):
            s_b = scores(hh, a, 2 * t + 1)
            m = softmax_pv(s_scr.at[hh, a], hh, a, 2 * t, m)
            s_scr[hh, a] = scores(hh, a, 2 * t + 2)
            out.append(softmax_pv(s_b, hh, a, 2 * t + 1, m))
        return tuple(out)

    for hh, a in chains:
        s_scr[hh, a] = scores(hh, a, 0)
    ms = lax.fori_loop(0, qi, pair, (jnp.full((1, tq), NEG_BIG, F32),) * len(chains))
    k_chunk = lax.broadcasted_iota(jnp.int32, (KEY_HALF, tq), 0) // CHUNK
    q_chunk = lax.broadcasted_iota(jnp.int32, (KEY_HALF, tq), 1) // CHUNK
    visible = k_chunk <= q_chunk
    late = slice(KEY_HALF, tq)
    for (hh, a), m in zip(chains, ms):
        s_b = scores(hh, a, 2 * qi + 1, late)
        m = softmax_pv(s_scr.at[hh, a], hh, a, 2 * qi, m, mask=visible)
        softmax_pv(s_b, hh, a, 2 * qi + 1, m[:, late], late, mask=visible)

    gain = ang_ref[...] * (1.0 - lam_init)
    for hh in range(HEADS_PER_STEP):
        lanes = slice(hh * V_DIM, (hh + 1) * V_DIM)
        o1 = acc_scr[hh, 0, :V_DIM, :] / acc_scr[hh, 0, V_DIM:V_DIM + 1, :]
        o2 = acc_scr[hh, 1, :V_DIM, :] / acc_scr[hh, 1, V_DIM:V_DIM + 1, :]
        o_t = o1 - lam * o2
        o_t = o_t * lax.rsqrt(jnp.mean(o_t * o_t, axis=0, keepdims=True) + EPS)
        o_ref[:, lanes] = (zb_ref[:, lanes].astype(F32) * (o_t.T * gain)).astype(BF16)


def _attn_prompt(lams, attn_norm_g, q, k, v, zb, *, lam_init, tq):
    b, s, _ = q.shape
    n_halves = s // KEY_HALF
    width = HEADS_PER_STEP * V_DIM
    lam_spec = pl.BlockSpec((1, HEAD_DIM), lambda bi, hi, qi: (0, 0))
    head_rows = pl.BlockSpec((None, tq, width), lambda bi, hi, qi: (bi, qi, hi))
    head_all = pl.BlockSpec((None, s, width), lambda bi, hi, qi: (bi, 0, hi))
    return pl.pallas_call(
        functools.partial(_attn_prompt_kernel, tq=tq, lam_init=lam_init),
        grid=(b, N_HEADS // HEADS_PER_STEP, s // tq),
        in_specs=[lam_spec] * 4 + [
            pl.BlockSpec((1, V_DIM), lambda bi, hi, qi: (0, 0)),
            head_rows, head_all, head_all, head_rows],
        out_specs=head_rows,
        out_shape=jax.ShapeDtypeStruct((b, s, D_MODEL), BF16),
        scratch_shapes=[
            pltpu.VMEM((HEADS_PER_STEP, n_halves, KEY_HALF, V_DIM), BF16),
            pltpu.VMEM((HEADS_PER_STEP, n_halves, VT_ROWS, KEY_HALF), BF16),
            pltpu.VMEM((HEADS_PER_STEP, 2, V_DIM, tq), BF16),
            pltpu.VMEM((HEADS_PER_STEP, 2, KEY_HALF, tq), F32),
            pltpu.VMEM((HEADS_PER_STEP, 2, VT_ROWS, tq), F32),
        ],
        compiler_params=pltpu.CompilerParams(
            dimension_semantics=("arbitrary", "arbitrary", "arbitrary"),
            vmem_limit_bytes=VMEM_LIMIT_BYTES),
        name="attn_prompt",
    )(*lams, attn_norm_g, q, k, v, zb)


def _attn_sample_kernel(lq1_ref, lk1_ref, lq2_ref, lk2_ref, ang_ref, q_ref, kn_ref, vn_ref,
                        kc_ref, vc_ref, zb_ref, o_ref, *, lam_init):
    lam = _lambda(lq1_ref, lk1_ref, lq2_ref, lk2_ref, lam_init)
    past = kc_ref.shape[0] // N_HEADS
    gain = ang_ref[...] * (1.0 - lam_init)
    comp = lax.broadcasted_iota(jnp.int32, (q_ref.shape[0], V_DIM), 1)
    contract_last = (((1,), (1,)), ((), ()))
    for h in range(N_HEADS):
        lanes = slice(h * V_DIM, (h + 1) * V_DIM)
        q = q_ref[:, lanes]
        zero = jnp.zeros_like(q)
        qz = (jnp.where(comp < HEAD_DIM, q, zero), jnp.where(comp >= HEAD_DIM, q, zero))
        k_c = kc_ref[pl.ds(h, past, stride=N_HEADS), :].astype(BF16)
        v_c = vc_ref[pl.ds(h, past, stride=N_HEADS), :].astype(BF16)
        k_n = kn_ref[:, lanes]
        v_n = vn_ref[:, lanes]
        outs = []
        for a in range(2):
            s_c = lax.dot_general(qz[a], k_c, contract_last, preferred_element_type=F32)
            s_n = lax.dot_general(qz[a], k_n, contract_last, preferred_element_type=F32)
            m = jnp.maximum(jnp.max(s_c, axis=-1, keepdims=True),
                            jnp.max(s_n, axis=-1, keepdims=True))
            p_c = jnp.exp2(s_c - m)
            p_n = jnp.exp2(s_n - m)
            l = jnp.sum(p_c, axis=-1, keepdims=True) + jnp.sum(p_n, axis=-1, keepdims=True)
            pv = (jnp.dot(p_c.astype(BF16), v_c, preferred_element_type=F32)
                  + jnp.dot(p_n.astype(BF16), v_n, preferred_element_type=F32))
            outs.append(pv / l)
        o = outs[0] - lam * outs[1]
        o_ref[:, lanes] = (zb_ref[:, lanes].astype(F32) * (_rms_scale(o) * gain)).astype(BF16)


def _attn_sample(lams, attn_norm_g, q, k_new, v_new, cache_k, cache_v, zb, *, layer, lam_init):
    b, l, _ = q.shape
    rows = cache_k.shape[2]
    lam_spec = pl.BlockSpec((1, HEAD_DIM), lambda bi: (0, 0))
    new_spec = pl.BlockSpec((None, l, D_MODEL), lambda bi: (bi, 0, 0))
    cache_spec = pl.BlockSpec((None, None, rows, V_DIM), lambda bi: (layer, bi, 0, 0))
    return pl.pallas_call(
        functools.partial(_attn_sample_kernel, lam_init=lam_init),
        grid=(b,),
        in_specs=[lam_spec] * 4 + [
            pl.BlockSpec((1, V_DIM), lambda bi: (0, 0)),
            new_spec, new_spec, new_spec, cache_spec, cache_spec, new_spec],
        out_specs=new_spec,
        out_shape=jax.ShapeDtypeStruct((b, l, D_MODEL), BF16),
        compiler_params=pltpu.CompilerParams(
            dimension_semantics=("arbitrary",), vmem_limit_bytes=VMEM_LIMIT_BYTES),
        name="attn_sample",
    )(*lams, attn_norm_g, q, k_new, v_new, cache_k, cache_v, zb)


def _outproj_kernel(x_ref, yb_ref, ma_ref, gb_ref, wpb_ref, wout_ref, fg_ref, o_ref, *, final):
    p_b = jnp.dot(yb_ref[...], wpb_ref[...], preferred_element_type=F32)
    merged = ma_ref[...].astype(F32) + gb_ref[...].astype(F32) * p_b
    x_new = x_ref[...] + jnp.dot(merged.astype(BF16), wout_ref[...], preferred_element_type=F32)
    if final:
        x_new = _rms_scale(x_new) * fg_ref[...]
    o_ref[...] = x_new


def _outproj(x2d, yb, ma, gb, w_pb_bf, w_out_bf, final_g, *, tm, final):
    m = x2d.shape[0]
    row_spec = pl.BlockSpec((tm, D_MODEL), lambda i: (i, 0))
    w_spec = pl.BlockSpec((D_MODEL, D_MODEL), lambda i: (0, 0))
    return pl.pallas_call(
        functools.partial(_outproj_kernel, final=final),
        grid=(m // tm,),
        in_specs=[row_spec, row_spec, row_spec, row_spec, w_spec, w_spec,
                  pl.BlockSpec((1, D_MODEL), lambda i: (0, 0))],
        out_specs=row_spec,
        out_shape=jax.ShapeDtypeStruct((m, D_MODEL), F32),
        compiler_params=pltpu.CompilerParams(
            dimension_semantics=("arbitrary",), vmem_limit_bytes=VMEM_LIMIT_BYTES),
        name="outproj",
    )(x2d, yb, ma, gb, w_pb_bf, w_out_bf, final_g)


def kernel(x_prompt, x_sample, cache_k, cache_v, norm_g, w_in, w_s, b_s, v_norm_g,
           lam_q1, lam_k1, lam_q2, lam_k2, attn_norm_g, w_pa, w_pb, w_out, final_norm_g):
    depth = w_in.shape[0]
    bp, sp, _ = x_prompt.shape
    bs_, ls, _ = x_sample.shape
    mlp_chunk = w_s.shape[-1]
    past = cache_k.shape[2]
    cache_k4 = cache_k.reshape(depth, bs_, past * N_HEADS, V_DIM)
    cache_v4 = cache_v.reshape(depth, bs_, past * N_HEADS, V_DIM)
    xp = x_prompt.reshape(bp * sp, D_MODEL)
    xs = x_sample.reshape(bs_ * ls, D_MODEL)
    final_g = final_norm_g.reshape(1, D_MODEL)
    hist_p, hist_s = (), ()
    for li in range(depth):
        lam_init = 0.8 - 0.6 * math.exp(-0.3 * li)
        final = li == depth - 1
        w_in_bf = w_in[li].astype(BF16)
        w_pa_bf = w_pa[li].astype(BF16)
        w_pb_bf = w_pb[li].astype(BF16)
        w_out_bf = w_out[li].astype(BF16)
        ng = norm_g[li].reshape(1, D_MODEL)
        vg = v_norm_g[li].reshape(1, D_MODEL)
        ang = attn_norm_g[li].reshape(1, V_DIM)
        lams = tuple(a[li].reshape(1, HEAD_DIM) for a in (lam_q1, lam_k1, lam_q2, lam_k2))
        bs_full = jnp.repeat(jnp.transpose(b_s[li]), GROUP_W, axis=1)

        q, kb, vb, zb, ma, gb, *hist_p = _inproj(
            xp, ng, w_in_bf, w_s[li], bs_full, vg, w_pa_bf, hist_p,
            t_chunk=mlp_chunk, tm=256, emit_av=False, layer=li)
        to_seq = lambda a: a.reshape(bp, sp, D_MODEL)
        yb = _attn_prompt(lams, ang, to_seq(q), to_seq(kb), to_seq(vb), to_seq(zb),
                          lam_init=lam_init, tq=512)
        xp = _outproj(xp, yb.reshape(bp * sp, D_MODEL), ma, gb, w_pb_bf, w_out_bf, final_g,
                      tm=512, final=final)

        q, kb, vb, zb, ma, gb, *hist_s = _inproj(
            xs, ng, w_in_bf, w_s[li][:, :ls, :ls], bs_full[:ls], vg, w_pa_bf, hist_s,
            t_chunk=ls, tm=256, emit_av=True, layer=li)
        to_seq = lambda a: a.reshape(bs_, ls, D_MODEL)
        yb = _attn_sample(lams, ang, to_seq(q), to_seq(kb), to_seq(vb), cache_k4, cache_v4,
                          to_seq(zb), layer=li, lam_init=lam_init)
        xs = _outproj(xs, yb.reshape(bs_ * ls, D_MODEL), ma, gb, w_pb_bf, w_out_bf, final_g,
                      tm=256, final=final)

    y_prompt = xp.reshape(bp, sp, D_MODEL)
    y_sample = xs.reshape(bs_, ls, D_MODEL)
    new_k_prompt = hist_p[0].reshape(depth, bp, sp, N_HEADS, V_DIM)
    new_v_prompt = hist_p[1].reshape(depth, bp, sp, N_HEADS, V_DIM)
    new_k_sample = hist_s[0].reshape(depth, bs_, ls, N_HEADS, V_DIM)
    new_v_sample = hist_s[1].reshape(depth, bs_, ls, N_HEADS, V_DIM)
    new_mlpv_sample = hist_s[2].reshape(depth, bs_, ls, D_MODEL)
    return (y_prompt, y_sample, new_k_prompt, new_v_prompt, new_k_sample, new_v_sample,
            new_mlpv_sample)
```

```python
import functools
import math

import jax
import jax.numpy as jnp
from jax import lax
from jax.experimental import pallas as pl
from jax.experimental.pallas import tpu as pltpu

F32 = jnp.float32
BF16 = jnp.bfloat16

D_MODEL = 1024
N_GROUPS = 4
GROUP_W = D_MODEL // N_GROUPS
N_HEADS = 8
HEAD_DIM = 64
V_DIM = 2 * HEAD_DIM
CHUNK = 64
EPS = 1e-6
N_SPLITS = 9
NEG_BIG = -1e30
Q_SCALE = HEAD_DIM ** -0.5 * math.log2(math.e)
SQRT_2_OVER_PI = math.sqrt(2.0 / math.pi)

VMEM_LIMIT_BYTES = 56 * 1024 * 1024


def _gelu_tanh(x):
    return x * (0.5 * (1.0 + jnp.tanh(SQRT_2_OVER_PI * (x + 0.044715 * (x * x * x)))))


def _sigmoid(x):
    return 1.0 / (1.0 + jnp.exp(-x))


def _silu(x):
    return x * _sigmoid(x)


def _rms_scale(x):
    return x * lax.rsqrt(jnp.mean(x * x, axis=-1, keepdims=True) + EPS)


def _store_heads(o_ref, val):
    tm = val.shape[0]
    for h in range(N_HEADS):
        o_ref[pl.ds(h, tm, stride=N_HEADS), :] = val[:, h * V_DIM:(h + 1) * V_DIM]


def _inproj_kernel(x_ref, ng_ref, win_ref, ws_ref, bs_ref, vg_ref, wpa_ref, *rest,
                   t_chunk, emit_av, layer):
    n_hist = 3 if emit_av else 2
    prev = rest[:n_hist] if layer else ()
    rest = rest[len(prev):]
    q_ref, kb_ref, vb_ref, zb_ref, ma_ref, gb_ref = rest[:6]
    hist = rest[6:6 + n_hist]
    vb_scr, sp_scr = rest[6 + n_hist:]
    for old, new in zip(prev, hist):
        new[:layer] = old[...]
    k_ref, v_ref = hist[0].at[layer], hist[1].at[layer]
    if emit_av:
        av_ref = hist[2].at[layer]
    tm = x_ref.shape[0]
    h = (_rms_scale(x_ref[...]) * ng_ref[...]).astype(BF16)

    def proj(j):
        return jnp.dot(h, win_ref[:, j * D_MODEL:(j + 1) * D_MODEL], preferred_element_type=F32)

    a_v = _rms_scale(_gelu_tanh(proj(1))) * vg_ref[...]
    if emit_av:
        av_ref[...] = a_v
    vb_scr[...] = a_v.astype(BF16)
    row = lax.broadcasted_iota(jnp.int32, (t_chunk, t_chunk), 0)
    col = lax.broadcasted_iota(jnp.int32, (t_chunk, t_chunk), 1)
    for g in range(N_GROUPS):
        w_g = jnp.where(col <= row, ws_ref[g], 0.0).astype(BF16)
        lanes = slice(g * GROUP_W, (g + 1) * GROUP_W)
        for c in range(tm // t_chunk):
            rows = slice(c * t_chunk, (c + 1) * t_chunk)
            sp_scr[rows, lanes] = (
                jnp.dot(w_g, vb_scr[rows, lanes], preferred_element_type=F32) + bs_ref[:, lanes])
    a_u = _gelu_tanh(proj(0))
    y_a = (_silu(proj(2)) * (a_u * sp_scr[...])).astype(BF16)
    p_a = jnp.dot(y_a, wpa_ref[...], preferred_element_type=F32)
    ma_ref[...] = (_sigmoid(proj(7)) * p_a).astype(BF16)
    q_ref[...] = (proj(3) * Q_SCALE).astype(BF16)
    b_k = proj(4)
    kb_ref[...] = b_k.astype(BF16)
    _store_heads(k_ref, b_k)
    b_v = proj(5)
    vb_ref[...] = b_v.astype(BF16)
    _store_heads(v_ref, b_v)
    zb_ref[...] = _silu(proj(6)).astype(BF16)
    gb_ref[...] = _sigmoid(proj(8)).astype(BF16)


def _inproj(x2d, norm_g, w_in_bf, w_s, bs_full, v_norm_g, w_pa_bf, prev, *, t_chunk, tm, emit_av,
            layer):
    m = x2d.shape[0]
    row_spec = pl.BlockSpec((tm, D_MODEL), lambda i: (i, 0))
    full = lambda shape: pl.BlockSpec(shape, lambda i: (0,) * len(shape))
    hist_tiles = [(tm * N_HEADS, V_DIM), (tm * N_HEADS, V_DIM)] + ([(tm, D_MODEL)] if emit_av else [])
    hist_rows = [m * N_HEADS, m * N_HEADS] + ([m] if emit_av else [])
    hist_spec = lambda n, tile: pl.BlockSpec((n,) + tile, lambda i: (0, i, 0))
    bf16_out = jax.ShapeDtypeStruct((m, D_MODEL), BF16)
    out_shape = [bf16_out] * 6 + [
        jax.ShapeDtypeStruct((layer + 1, rows, tile[1]), F32)
        for rows, tile in zip(hist_rows, hist_tiles)]
    out_specs = [row_spec] * 6 + [hist_spec(layer + 1, tile) for tile in hist_tiles]
    prev_specs = [hist_spec(layer, tile) for tile in hist_tiles] if layer else []
    return pl.pallas_call(
        functools.partial(_inproj_kernel, t_chunk=t_chunk, emit_av=emit_av, layer=layer),
        grid=(m // tm,),
        in_specs=[
            row_spec,
            full((1, D_MODEL)),
            full((D_MODEL, N_SPLITS * D_MODEL)),
            full((N_GROUPS, t_chunk, t_chunk)),
            full((t_chunk, D_MODEL)),
            full((1, D_MODEL)),
            full((D_MODEL, D_MODEL)),
        ] + prev_specs,
        out_specs=out_specs,
        out_shape=out_shape,
        scratch_shapes=[pltpu.VMEM((tm, D_MODEL), BF16), pltpu.VMEM((tm, D_MODEL), F32)],
        compiler_params=pltpu.CompilerParams(
            dimension_semantics=("arbitrary",), vmem_limit_bytes=VMEM_LIMIT_BYTES),
        name="inproj",
    )(x2d, norm_g, w_in_bf, w_s, bs_full, v_norm_g, w_pa_bf, *prev)


def _lambda(lq1_ref, lk1_ref, lq2_ref, lk2_ref, lam_init):
    s1 = jnp.sum(lq1_ref[...] * lk1_ref[...], axis=-1, keepdims=True)
    s2 = jnp.sum(lq2_ref[...] * lk2_ref[...], axis=-1, keepdims=True)
    return jnp.exp(s1) - jnp.exp(s2) + lam_init


KEY_HALF = 256
HEADS_PER_STEP = 2


def _attn_prompt_kernel(lq1_ref, lk1_ref, lq2_ref, lk2_ref, ang_ref, q_ref, k_ref, v_ref, zb_ref,
                        o_ref, kb_scr, vt_scr, qz_scr, s_scr, e_scr, mj_scr, acc_scr,
                        *, tq, lam_init):
    qi = pl.program_id(2)
    n_halves = kb_scr.shape[1]
    assert tq == 2 * KEY_HALF
    chains = [(hh, a) for hh in range(HEADS_PER_STEP) for a in range(2)]

    @pl.when(qi == 0)
    def _():
        for hh in range(HEADS_PER_STEP):
            lanes = slice(hh * V_DIM, (hh + 1) * V_DIM)
            for c in range(n_halves):
                rows = slice(c * KEY_HALF, (c + 1) * KEY_HALF)
                kb_scr[hh, c] = k_ref[rows, lanes]
                vt_scr[hh, c] = v_ref[rows, lanes].astype(F32).T.astype(BF16)

    lam = _lambda(lq1_ref, lk1_ref, lq2_ref, lk2_ref, lam_init)
    comp = lax.broadcasted_iota(jnp.int32, (V_DIM, tq), 0)
    for hh in range(HEADS_PER_STEP):
        q_t = q_ref[:, hh * V_DIM:(hh + 1) * V_DIM].astype(F32).T
        qz_scr[hh, 0] = jnp.where(comp < HEAD_DIM, q_t, 0.0).astype(BF16)
        qz_scr[hh, 1] = jnp.where(comp >= HEAD_DIM, q_t, 0.0).astype(BF16)
    acc_scr[...] = jnp.zeros_like(acc_scr)

    def scores(hh, a, j, lanes=slice(None)):
        return jnp.dot(kb_scr[hh, j], qz_scr[hh, a, :, lanes], preferred_element_type=F32)

    def stats(s_t, hh, a, j, m_old, l_old, lanes=slice(None), mask=None):
        if mask is not None:
            s_t = jnp.where(mask, s_t, NEG_BIG)
        m_new = jnp.maximum(m_old, jnp.max(s_t, axis=0, keepdims=True))
        e_t = jnp.exp2(s_t - m_new)
        l_new = jnp.exp2(m_old - m_new) * l_old + jnp.sum(e_t, axis=0, keepdims=True)
        e_scr[hh, a, j, :, lanes] = e_t.astype(BF16)
        mj_scr[hh, a, j, :, lanes] = m_new
        return m_new, l_new

    def pair_stats(t, st):
        out = []
        for i, (hh, a) in enumerate(chains):
            m, l = st[2 * i], st[2 * i + 1]
            s_b = scores(hh, a, 2 * t + 1)
            m, l = stats(s_scr[hh, a], hh, a, 2 * t, m, l)
            s_scr[hh, a] = scores(hh, a, 2 * t + 2)
            m, l = stats(s_b, hh, a, 2 * t + 1, m, l)
            out += [m, l]
        return tuple(out)

    for hh, a in chains:
        s_scr[hh, a] = scores(hh, a, 0)
    init = (jnp.full((1, tq), NEG_BIG, F32), jnp.zeros((1, tq), F32)) * len(chains)
    st = lax.fori_loop(0, qi, pair_stats, init)
    k_chunk = lax.broadcasted_iota(jnp.int32, (KEY_HALF, tq), 0) // CHUNK
    q_chunk = lax.broadcasted_iota(jnp.int32, (KEY_HALF, tq), 1) // CHUNK
    visible = k_chunk <= q_chunk
    early, late = slice(0, KEY_HALF), slice(KEY_HALF, tq)
    m_fin, w_fin = {}, {}
    for i, (hh, a) in enumerate(chains):
        m, l = st[2 * i], st[2 * i + 1]
        s_b = scores(hh, a, 2 * qi + 1, late)
        m, l = stats(s_scr[hh, a], hh, a, 2 * qi, m, l, mask=visible)
        m_late, l_late = stats(s_b, hh, a, 2 * qi + 1, m[:, late], l[:, late], late,
                               mask=visible[:, early])
        m_fin[hh, a] = jnp.concatenate([m[:, early], m_late], axis=1)
        l_all = jnp.concatenate([l[:, early], l_late], axis=1)
        w_fin[hh, a] = (1.0 if a == 0 else lam) / l_all

    def weights_pv(hh, j, lanes=slice(None)):
        c = [(jnp.exp2(mj_scr[hh, a, j, :, lanes] - m_fin[hh, a][:, lanes])
              * w_fin[hh, a][:, lanes]).astype(BF16) for a in range(2)]
        a_t = e_scr[hh, 0, j, :, lanes] * c[0] - e_scr[hh, 1, j, :, lanes] * c[1]
        acc_scr[hh, :, lanes] += jnp.dot(vt_scr[hh, j], a_t, preferred_element_type=F32)

    def pair_pv(t, carry):
        for hh in range(HEADS_PER_STEP):
            weights_pv(hh, 2 * t)
            weights_pv(hh, 2 * t + 1)
        return carry

    lax.fori_loop(0, qi, pair_pv, 0)
    gain = ang_ref[...] * (1.0 - lam_init)
    for hh in range(HEADS_PER_STEP):
        lanes = slice(hh * V_DIM, (hh + 1) * V_DIM)
        weights_pv(hh, 2 * qi)
        weights_pv(hh, 2 * qi + 1, late)
        o_t = acc_scr[hh]
        o_t = o_t * lax.rsqrt(jnp.mean(o_t * o_t, axis=0, keepdims=True) + EPS)
        o_ref[:, lanes] = (zb_ref[:, lanes].astype(F32) * (o_t.T * gain)).astype(BF16)


def _attn_prompt(lams, attn_norm_g, q, k, v, zb, *, lam_init, tq):
    b, s, _ = q.shape
    n_halves = s // KEY_HALF
    width = HEADS_PER_STEP * V_DIM
    lam_spec = pl.BlockSpec((1, HEAD_DIM), lambda bi, hi, qi: (0, 0))
    head_rows = pl.BlockSpec((None, tq, width), lambda bi, hi, qi: (bi, qi, hi))
    head_all = pl.BlockSpec((None, s, width), lambda bi, hi, qi: (bi, 0, hi))
    return pl.pallas_call(
        functools.partial(_attn_prompt_kernel, tq=tq, lam_init=lam_init),
        grid=(b, N_HEADS // HEADS_PER_STEP, s // tq),
        in_specs=[lam_spec] * 4 + [
            pl.BlockSpec((1, V_DIM), lambda bi, hi, qi: (0, 0)),
            head_rows, head_all, head_all, head_rows],
        out_specs=head_rows,
        out_shape=jax.ShapeDtypeStruct((b, s, D_MODEL), BF16),
        scratch_shapes=[
            pltpu.VMEM((HEADS_PER_STEP, n_halves, KEY_HALF, V_DIM), BF16),
            pltpu.VMEM((HEADS_PER_STEP, n_halves, V_DIM, KEY_HALF), BF16),
            pltpu.VMEM((HEADS_PER_STEP, 2, V_DIM, tq), BF16),
            pltpu.VMEM((HEADS_PER_STEP, 2, KEY_HALF, tq), F32),
            pltpu.VMEM((HEADS_PER_STEP, 2, n_halves, KEY_HALF, tq), BF16),
            pltpu.VMEM((HEADS_PER_STEP, 2, n_halves, 1, tq), F32),
            pltpu.VMEM((HEADS_PER_STEP, V_DIM, tq), F32),
        ],
        compiler_params=pltpu.CompilerParams(
            dimension_semantics=("arbitrary", "arbitrary", "arbitrary"),
            vmem_limit_bytes=VMEM_LIMIT_BYTES),
        name="attn_prompt",
    )(*lams, attn_norm_g, q, k, v, zb)


def _attn_sample_kernel(lq1_ref, lk1_ref, lq2_ref, lk2_ref, ang_ref, q_ref, kn_ref, vn_ref,
                        kc_ref, vc_ref, zb_ref, o_ref, *, lam_init):
    lam = _lambda(lq1_ref, lk1_ref, lq2_ref, lk2_ref, lam_init)
    past = kc_ref.shape[0] // N_HEADS
    n_new = q_ref.shape[0]
    gain = ang_ref[...] * (1.0 - lam_init)
    comp = lax.broadcasted_iota(jnp.int32, (n_new, V_DIM), 1)
    contract_last = (((1,), (1,)), ((), ()))
    for h in range(N_HEADS):
        lanes = slice(h * V_DIM, (h + 1) * V_DIM)
        q = q_ref[:, lanes]
        zero = jnp.zeros_like(q)
        qz = jnp.concatenate([jnp.where(comp < HEAD_DIM, q, zero),
                              jnp.where(comp >= HEAD_DIM, q, zero)], axis=0)
        k_c = kc_ref[pl.ds(h, past, stride=N_HEADS), :].astype(BF16)
        v_c = vc_ref[pl.ds(h, past, stride=N_HEADS), :].astype(BF16)
        s_c = lax.dot_general(qz, k_c, contract_last, preferred_element_type=F32)
        s_n = lax.dot_general(qz, kn_ref[:, lanes], contract_last, preferred_element_type=F32)
        m = jnp.maximum(jnp.max(s_c, axis=-1, keepdims=True), jnp.max(s_n, axis=-1, keepdims=True))
        p_c = jnp.exp2(s_c - m)
        p_n = jnp.exp2(s_n - m)
        l = jnp.sum(p_c, axis=-1, keepdims=True) + jnp.sum(p_n, axis=-1, keepdims=True)
        pv = (jnp.dot(p_c.astype(BF16), v_c, preferred_element_type=F32)
              + jnp.dot(p_n.astype(BF16), vn_ref[:, lanes], preferred_element_type=F32)) / l
        o = pv[:n_new] - lam * pv[n_new:]
        o_ref[:, lanes] = (zb_ref[:, lanes].astype(F32) * (_rms_scale(o) * gain)).astype(BF16)


def _attn_sample(lams, attn_norm_g, q, k_new, v_new, cache_k, cache_v, zb, *, layer, lam_init):
    b, l, _ = q.shape
    rows = cache_k.shape[2]
    lam_spec = pl.BlockSpec((1, HEAD_DIM), lambda bi: (0, 0))
    new_spec = pl.BlockSpec((None, l, D_MODEL), lambda bi: (bi, 0, 0))
    cache_spec = pl.BlockSpec((None, None, rows, V_DIM), lambda bi: (layer, bi, 0, 0))
    return pl.pallas_call(
        functools.partial(_attn_sample_kernel, lam_init=lam_init),
        grid=(b,),
        in_specs=[lam_spec] * 4 + [
            pl.BlockSpec((1, V_DIM), lambda bi: (0, 0)),
            new_spec, new_spec, new_spec, cache_spec, cache_spec, new_spec],
        out_specs=new_spec,
        out_shape=jax.ShapeDtypeStruct((b, l, D_MODEL), BF16),
        compiler_params=pltpu.CompilerParams(
            dimension_semantics=("arbitrary",), vmem_limit_bytes=VMEM_LIMIT_BYTES),
        name="attn_sample",
    )(*lams, attn_norm_g, q, k_new, v_new, cache_k, cache_v, zb)


def _outproj_kernel(x_ref, yb_ref, ma_ref, gb_ref, wpb_ref, wout_ref, fg_ref, o_ref, *, final):
    p_b = jnp.dot(yb_ref[...], wpb_ref[...], preferred_element_type=F32)
    merged = ma_ref[...].astype(F32) + gb_ref[...].astype(F32) * p_b
    x_new = x_ref[...] + jnp.dot(merged.astype(BF16), wout_ref[...], preferred_element_type=F32)
    if final:
        x_new = _rms_scale(x_new) * fg_ref[...]
    o_ref[...] = x_new


def _outproj(x2d, yb, ma, gb, w_pb_bf, w_out_bf, final_g, *, tm, final):
    m = x2d.shape[0]
    row_spec = pl.BlockSpec((tm, D_MODEL), lambda i: (i, 0))
    w_spec = pl.BlockSpec((D_MODEL, D_MODEL), lambda i: (0, 0))
    return pl.pallas_call(
        functools.partial(_outproj_kernel, final=final),
        grid=(m // tm,),
        in_specs=[row_spec, row_spec, row_spec, row_spec, w_spec, w_spec,
                  pl.BlockSpec((1, D_MODEL), lambda i: (0, 0))],
        out_specs=row_spec,
        out_shape=jax.ShapeDtypeStruct((m, D_MODEL), F32),
        compiler_params=pltpu.CompilerParams(
            dimension_semantics=("arbitrary",), vmem_limit_bytes=VMEM_LIMIT_BYTES),
        name="outproj",
    )(x2d, yb, ma, gb, w_pb_bf, w_out_bf, final_g)


def kernel(x_prompt, x_sample, cache_k, cache_v, norm_g, w_in, w_s, b_s, v_norm_g,
           lam_q1, lam_k1, lam_q2, lam_k2, attn_norm_g, w_pa, w_pb, w_out, final_norm_g):
    depth = w_in.shape[0]
    bp, sp, _ = x_prompt.shape
    bs_, ls, _ = x_sample.shape
    mlp_chunk = w_s.shape[-1]
    past = cache_k.shape[2]
    cache_k4 = cache_k.reshape(depth, bs_, past * N_HEADS, V_DIM)
    cache_v4 = cache_v.reshape(depth, bs_, past * N_HEADS, V_DIM)
    xp = x_prompt.reshape(bp * sp, D_MODEL)
    xs = x_sample.reshape(bs_ * ls, D_MODEL)
    final_g = final_norm_g.reshape(1, D_MODEL)
    hist_p, hist_s = (), ()
    for li in range(depth):
        lam_init = 0.8 - 0.6 * math.exp(-0.3 * li)
        final = li == depth - 1
        w_in_bf = w_in[li].astype(BF16)
        w_pa_bf = w_pa[li].astype(BF16)
        w_pb_bf = w_pb[li].astype(BF16)
        w_out_bf = w_out[li].astype(BF16)
        ng = norm_g[li].reshape(1, D_MODEL)
        vg = v_norm_g[li].reshape(1, D_MODEL)
        ang = attn_norm_g[li].reshape(1, V_DIM)
        lams = tuple(a[li].reshape(1, HEAD_DIM) for a in (lam_q1, lam_k1, lam_q2, lam_k2))
        bs_full = jnp.repeat(jnp.transpose(b_s[li]), GROUP_W, axis=1)

        q, kb, vb, zb, ma, gb, *hist_p = _inproj(
            xp, ng, w_in_bf, w_s[li], bs_full, vg, w_pa_bf, hist_p,
            t_chunk=mlp_chunk, tm=256, emit_av=False, layer=li)
        to_seq = lambda a: a.reshape(bp, sp, D_MODEL)
        yb = _attn_prompt(lams, ang, to_seq(q), to_seq(kb), to_seq(vb), to_seq(zb),
                          lam_init=lam_init, tq=512)
        xp = _outproj(xp, yb.reshape(bp * sp, D_MODEL), ma, gb, w_pb_bf, w_out_bf, final_g,
                      tm=512, final=final)

        q, kb, vb, zb, ma, gb, *hist_s = _inproj(
            xs, ng, w_in_bf, w_s[li][:, :ls, :ls], bs_full[:ls], vg, w_pa_bf, hist_s,
            t_chunk=ls, tm=256, emit_av=True, layer=li)
        to_seq = lambda a: a.reshape(bs_, ls, D_MODEL)
        yb = _attn_sample(lams, ang, to_seq(q), to_seq(kb), to_seq(vb), cache_k4, cache_v4,
                          to_seq(zb), layer=li, lam_init=lam_init)
        xs = _outproj(xs, yb.reshape(bs_ * ls, D_MODEL), ma, gb, w_pb_bf, w_out_bf, final_g,
                      tm=256, final=final)

    y_prompt = xp.reshape(bp, sp, D_MODEL)
    y_sample = xs.reshape(bs_, ls, D_MODEL)
    new_k_prompt = hist_p[0].reshape(depth, bp, sp, N_HEADS, V_DIM)
    new_v_prompt = hist_p[1].reshape(depth, bp, sp, N_HEADS, V_DIM)
    new_k_sample = hist_s[0].reshape(depth, bs_, ls, N_HEADS, V_DIM)
    new_v_sample = hist_s[1].reshape(depth, bs_, ls, N_HEADS, V_DIM)
    new_mlpv_sample = hist_s[2].reshape(depth, bs_, ls, D_MODEL)
    return (y_prompt, y_sample, new_k_prompt, new_v_prompt, new_k_sample, new_v_sample,
            new_mlpv_sample)
```

```python
import functools
import math

import jax
import jax.numpy as jnp
from jax import lax
from jax.experimental import pallas as pl
from jax.experimental.pallas import tpu as pltpu

F32 = jnp.float32
BF16 = jnp.bfloat16

D_MODEL = 1024
N_GROUPS = 4
GROUP_W = D_MODEL // N_GROUPS
N_HEADS = 8
HEAD_DIM = 64
V_DIM = 2 * HEAD_DIM
CHUNK = 64
EPS = 1e-6
N_SPLITS = 9
NEG_BIG = -1e30
Q_SCALE = HEAD_DIM ** -0.5 * math.log2(math.e)
SQRT_2_OVER_PI = math.sqrt(2.0 / math.pi)

VMEM_LIMIT_BYTES = 56 * 1024 * 1024


def _gelu_tanh(x):
    return x * (0.5 * (1.0 + jnp.tanh(SQRT_2_OVER_PI * (x + 0.044715 * (x * x * x)))))


def _sigmoid(x):
    return 1.0 / (1.0 + jnp.exp(-x))


def _silu(x):
    return x * _sigmoid(x)


def _rms_scale(x):
    return x * lax.rsqrt(jnp.mean(x * x, axis=-1, keepdims=True) + EPS)


def _store_heads(o_ref, val):
    tm = val.shape[0]
    for h in range(N_HEADS):
        o_ref[pl.ds(h, tm, stride=N_HEADS), :] = val[:, h * V_DIM:(h + 1) * V_DIM]


def _inproj_kernel(x_ref, ng_ref, win_ref, ws_ref, bs_ref, vg_ref, wpa_ref, *rest,
                   t_chunk, emit_av, layer):
    n_hist = 3 if emit_av else 2
    prev = rest[:n_hist] if layer else ()
    rest = rest[len(prev):]
    q_ref, kb_ref, vb_ref, zb_ref, ma_ref, gb_ref = rest[:6]
    hist = rest[6:6 + n_hist]
    vb_scr, sp_scr = rest[6 + n_hist:]
    for old, new in zip(prev, hist):
        new[:layer] = old[...]
    k_ref, v_ref = hist[0].at[layer], hist[1].at[layer]
    if emit_av:
        av_ref = hist[2].at[layer]
    tm = x_ref.shape[0]
    h = (_rms_scale(x_ref[...]) * ng_ref[...]).astype(BF16)

    def proj(j):
        return jnp.dot(h, win_ref[:, j * D_MODEL:(j + 1) * D_MODEL], preferred_element_type=F32)

    a_v = _rms_scale(_gelu_tanh(proj(1))) * vg_ref[...]
    if emit_av:
        av_ref[...] = a_v
    vb_scr[...] = a_v.astype(BF16)
    row = lax.broadcasted_iota(jnp.int32, (t_chunk, t_chunk), 0)
    col = lax.broadcasted_iota(jnp.int32, (t_chunk, t_chunk), 1)
    for g in range(N_GROUPS):
        w_g = jnp.where(col <= row, ws_ref[g], 0.0).astype(BF16)
        lanes = slice(g * GROUP_W, (g + 1) * GROUP_W)
        for c in range(tm // t_chunk):
            rows = slice(c * t_chunk, (c + 1) * t_chunk)
            sp_scr[rows, lanes] = (
                jnp.dot(w_g, vb_scr[rows, lanes], preferred_element_type=F32) + bs_ref[:, lanes])
    a_u = _gelu_tanh(proj(0))
    y_a = (_silu(proj(2)) * (a_u * sp_scr[...])).astype(BF16)
    p_a = jnp.dot(y_a, wpa_ref[...], preferred_element_type=F32)
    ma_ref[...] = (_sigmoid(proj(7)) * p_a).astype(BF16)
    q_ref[...] = (proj(3) * Q_SCALE).astype(BF16)
    b_k = proj(4)
    kb_ref[...] = b_k.astype(BF16)
    _store_heads(k_ref, b_k)
    b_v = proj(5)
    vb_ref[...] = b_v.astype(BF16)
    _store_heads(v_ref, b_v)
    zb_ref[...] = _silu(proj(6)).astype(BF16)
    gb_ref[...] = _sigmoid(proj(8)).astype(BF16)


def _inproj(x2d, norm_g, w_in_bf, w_s, bs_full, v_norm_g, w_pa_bf, prev, *, t_chunk, tm, emit_av,
            layer):
    m = x2d.shape[0]
    row_spec = pl.BlockSpec((tm, D_MODEL), lambda i: (i, 0))
    full = lambda shape: pl.BlockSpec(shape, lambda i: (0,) * len(shape))
    hist_tiles = [(tm * N_HEADS, V_DIM), (tm * N_HEADS, V_DIM)] + ([(tm, D_MODEL)] if emit_av else [])
    hist_rows = [m * N_HEADS, m * N_HEADS] + ([m] if emit_av else [])
    hist_spec = lambda n, tile: pl.BlockSpec((n,) + tile, lambda i: (0, i, 0))
    bf16_out = jax.ShapeDtypeStruct((m, D_MODEL), BF16)
    out_shape = [bf16_out] * 6 + [
        jax.ShapeDtypeStruct((layer + 1, rows, tile[1]), F32)
        for rows, tile in zip(hist_rows, hist_tiles)]
    out_specs = [row_spec] * 6 + [hist_spec(layer + 1, tile) for tile in hist_tiles]
    prev_specs = [hist_spec(layer, tile) for tile in hist_tiles] if layer else []
    return pl.pallas_call(
        functools.partial(_inproj_kernel, t_chunk=t_chunk, emit_av=emit_av, layer=layer),
        grid=(m // tm,),
        in_specs=[
            row_spec,
            full((1, D_MODEL)),
            full((D_MODEL, N_SPLITS * D_MODEL)),
            full((N_GROUPS, t_chunk, t_chunk)),
            full((t_chunk, D_MODEL)),
            full((1, D_MODEL)),
            full((D_MODEL, D_MODEL)),
        ] + prev_specs,
        out_specs=out_specs,
        out_shape=out_shape,
        scratch_shapes=[pltpu.VMEM((tm, D_MODEL), BF16), pltpu.VMEM((tm, D_MODEL), F32)],
        compiler_params=pltpu.CompilerParams(
            dimension_semantics=("arbitrary",), vmem_limit_bytes=VMEM_LIMIT_BYTES),
        name="inproj",
    )(x2d, norm_g, w_in_bf, w_s, bs_full, v_norm_g, w_pa_bf, *prev)


def _lambda(lq1_ref, lk1_ref, lq2_ref, lk2_ref, lam_init):
    s1 = jnp.sum(lq1_ref[...] * lk1_ref[...], axis=-1, keepdims=True)
    s2 = jnp.sum(lq2_ref[...] * lk2_ref[...], axis=-1, keepdims=True)
    return jnp.exp(s1) - jnp.exp(s2) + lam_init


KEY_HALF = 256
HEADS_PER_STEP = 1


def _attn_prompt_kernel(lq1_ref, lk1_ref, lq2_ref, lk2_ref, ang_ref, q_ref, k_ref, v_ref, zb_ref,
                        o_ref, kb_scr, vt_scr, qz_scr, s_scr, e_scr, mj_scr, acc_scr,
                        *, tq, lam_init):
    qi = pl.program_id(2)
    n_halves = kb_scr.shape[1]
    n_sub = tq // KEY_HALF
    chains = [(hh, a) for hh in range(HEADS_PER_STEP) for a in range(2)]

    @pl.when(qi == 0)
    def _():
        for hh in range(HEADS_PER_STEP):
            lanes = slice(hh * V_DIM, (hh + 1) * V_DIM)
            for c in range(n_halves):
                rows = slice(c * KEY_HALF, (c + 1) * KEY_HALF)
                kb_scr[hh, c] = k_ref[rows, lanes]
                vt_scr[hh, c] = v_ref[rows, lanes].astype(F32).T.astype(BF16)

    lam = _lambda(lq1_ref, lk1_ref, lq2_ref, lk2_ref, lam_init)
    comp = lax.broadcasted_iota(jnp.int32, (V_DIM, tq), 0)
    for hh in range(HEADS_PER_STEP):
        q_t = q_ref[:, hh * V_DIM:(hh + 1) * V_DIM].astype(F32).T
        qz_scr[hh, 0] = jnp.where(comp < HEAD_DIM, q_t, 0.0).astype(BF16)
        qz_scr[hh, 1] = jnp.where(comp >= HEAD_DIM, q_t, 0.0).astype(BF16)
    acc_scr[...] = jnp.zeros_like(acc_scr)

    def scores(hh, a, j, lanes=slice(None)):
        return jnp.dot(kb_scr[hh, j], qz_scr[hh, a, :, lanes], preferred_element_type=F32)

    def stats(s_t, hh, a, j, m_old, l_old, lanes=slice(None)):
        m_new = jnp.maximum(m_old, jnp.max(s_t, axis=0, keepdims=True))
        e_t = jnp.exp2(s_t - m_new)
        l_new = jnp.exp2(m_old - m_new) * l_old + jnp.sum(e_t, axis=0, keepdims=True)
        e_scr[hh, a, j, :, lanes] = e_t.astype(BF16)
        mj_scr[hh, a, j, :, lanes] = m_new
        return m_new, l_new

    def tile_lanes(d):
        return slice(d * KEY_HALF, tq)

    def group_stats(t, st):
        out = []
        for i, (hh, a) in enumerate(chains):
            m, l = st[2 * i], st[2 * i + 1]
            later = [scores(hh, a, n_sub * t + d) for d in range(1, n_sub)]
            m, l = stats(s_scr[hh, a], hh, a, n_sub * t, m, l)
            s_scr[hh, a] = scores(hh, a, n_sub * (t + 1))
            for d, s_d in enumerate(later, start=1):
                m, l = stats(s_d, hh, a, n_sub * t + d, m, l)
            out += [m, l]
        return tuple(out)

    for hh, a in chains:
        s_scr[hh, a] = scores(hh, a, 0)
    init = (jnp.full((1, tq), NEG_BIG, F32), jnp.zeros((1, tq), F32)) * len(chains)
    st = lax.fori_loop(0, qi, group_stats, init)
    k_chunk = lax.broadcasted_iota(jnp.int32, (KEY_HALF, KEY_HALF), 0) // CHUNK
    q_chunk = lax.broadcasted_iota(jnp.int32, (KEY_HALF, KEY_HALF), 1) // CHUNK
    triangle = k_chunk <= q_chunk

    def diag_scores(s_t):
        head = jnp.where(triangle, s_t[:, :KEY_HALF], NEG_BIG)
        return head if s_t.shape[1] == KEY_HALF else jnp.concatenate(
            [head, s_t[:, KEY_HALF:]], axis=1)

    m_fin, w_fin = {}, {}
    for i, (hh, a) in enumerate(chains):
        m, l = st[2 * i], st[2 * i + 1]
        later = [scores(hh, a, n_sub * qi + d, tile_lanes(d)) for d in range(1, n_sub)]
        m, l = stats(diag_scores(s_scr[hh, a]), hh, a, n_sub * qi, m, l)
        for d, s_d in enumerate(later, start=1):
            lanes = tile_lanes(d)
            m_d, l_d = stats(diag_scores(s_d), hh, a, n_sub * qi + d, m[:, lanes], l[:, lanes],
                             lanes)
            m = jnp.concatenate([m[:, :d * KEY_HALF], m_d], axis=1)
            l = jnp.concatenate([l[:, :d * KEY_HALF], l_d], axis=1)
        m_fin[hh, a] = m
        w_fin[hh, a] = (1.0 if a == 0 else lam) / l

    def weights_pv(hh, j, lanes=slice(None)):
        c = [(jnp.exp2(mj_scr[hh, a, j, :, lanes] - m_fin[hh, a][:, lanes])
              * w_fin[hh, a][:, lanes]).astype(BF16) for a in range(2)]
        a_t = e_scr[hh, 0, j, :, lanes] * c[0] - e_scr[hh, 1, j, :, lanes] * c[1]
        acc_scr[hh, :, lanes] += jnp.dot(vt_scr[hh, j], a_t, preferred_element_type=F32)

    def group_pv(t, carry):
        for hh in range(HEADS_PER_STEP):
            for d in range(n_sub):
                weights_pv(hh, n_sub * t + d)
        return carry

    lax.fori_loop(0, qi, group_pv, 0)
    gain = ang_ref[...] * (1.0 - lam_init)
    for hh in range(HEADS_PER_STEP):
        lanes = slice(hh * V_DIM, (hh + 1) * V_DIM)
        for d in range(n_sub):
            weights_pv(hh, n_sub * qi + d, tile_lanes(d))
        o_t = acc_scr[hh]
        o_t = o_t * lax.rsqrt(jnp.mean(o_t * o_t, axis=0, keepdims=True) + EPS)
        o_ref[:, lanes] = (zb_ref[:, lanes].astype(F32) * (o_t.T * gain)).astype(BF16)


def _attn_prompt(lams, attn_norm_g, q, k, v, zb, *, lam_init, tq):
    b, s, _ = q.shape
    n_halves = s // KEY_HALF
    width = HEADS_PER_STEP * V_DIM
    lam_spec = pl.BlockSpec((1, HEAD_DIM), lambda bi, hi, qi: (0, 0))
    head_rows = pl.BlockSpec((None, tq, width), lambda bi, hi, qi: (bi, qi, hi))
    head_all = pl.BlockSpec((None, s, width), lambda bi, hi, qi: (bi, 0, hi))
    return pl.pallas_call(
        functools.partial(_attn_prompt_kernel, tq=tq, lam_init=lam_init),
        grid=(b, N_HEADS // HEADS_PER_STEP, s // tq),
        in_specs=[lam_spec] * 4 + [
            pl.BlockSpec((1, V_DIM), lambda bi, hi, qi: (0, 0)),
            head_rows, head_all, head_all, head_rows],
        out_specs=head_rows,
        out_shape=jax.ShapeDtypeStruct((b, s, D_MODEL), BF16),
        scratch_shapes=[
            pltpu.VMEM((HEADS_PER_STEP, n_halves, KEY_HALF, V_DIM), BF16),
            pltpu.VMEM((HEADS_PER_STEP, n_halves, V_DIM, KEY_HALF), BF16),
            pltpu.VMEM((HEADS_PER_STEP, 2, V_DIM, tq), BF16),
            pltpu.VMEM((HEADS_PER_STEP, 2, KEY_HALF, tq), F32),
            pltpu.VMEM((HEADS_PER_STEP, 2, n_halves, KEY_HALF, tq), BF16),
            pltpu.VMEM((HEADS_PER_STEP, 2, n_halves, 1, tq), F32),
            pltpu.VMEM((HEADS_PER_STEP, V_DIM, tq), F32),
        ],
        compiler_params=pltpu.CompilerParams(
            dimension_semantics=("arbitrary", "arbitrary", "arbitrary"),
            vmem_limit_bytes=VMEM_LIMIT_BYTES),
        name="attn_prompt",
    )(*lams, attn_norm_g, q, k, v, zb)


def _attn_sample_kernel(lq1_ref, lk1_ref, lq2_ref, lk2_ref, ang_ref, q_ref, kn_ref, vn_ref,
                        kc_ref, vc_ref, zb_ref, o_ref, *, lam_init):
    lam = _lambda(lq1_ref, lk1_ref, lq2_ref, lk2_ref, lam_init)
    past = kc_ref.shape[0] // N_HEADS
    n_new = q_ref.shape[0]
    gain = ang_ref[...] * (1.0 - lam_init)
    comp = lax.broadcasted_iota(jnp.int32, (n_new, V_DIM), 1)
    contract_last = (((1,), (1,)), ((), ()))
    for h in range(N_HEADS):
        lanes = slice(h * V_DIM, (h + 1) * V_DIM)
        q = q_ref[:, lanes]
        zero = jnp.zeros_like(q)
        qz = jnp.concatenate([jnp.where(comp < HEAD_DIM, q, zero),
                              jnp.where(comp >= HEAD_DIM, q, zero)], axis=0)
        k_c = kc_ref[pl.ds(h, past, stride=N_HEADS), :].astype(BF16)
        v_c = vc_ref[pl.ds(h, past, stride=N_HEADS), :].astype(BF16)
        s_c = lax.dot_general(qz, k_c, contract_last, preferred_element_type=F32)
        s_n = lax.dot_general(qz, kn_ref[:, lanes], contract_last, preferred_element_type=F32)
        m = jnp.maximum(jnp.max(s_c, axis=-1, keepdims=True), jnp.max(s_n, axis=-1, keepdims=True))
        p_c = jnp.exp2(s_c - m)
        p_n = jnp.exp2(s_n - m)
        l = jnp.sum(p_c, axis=-1, keepdims=True) + jnp.sum(p_n, axis=-1, keepdims=True)
        pv = (jnp.dot(p_c.astype(BF16), v_c, preferred_element_type=F32)
              + jnp.dot(p_n.astype(BF16), vn_ref[:, lanes], preferred_element_type=F32)) / l
        o = pv[:n_new] - lam * pv[n_new:]
        o_ref[:, lanes] = (zb_ref[:, lanes].astype(F32) * (_rms_scale(o) * gain)).astype(BF16)


def _attn_sample(lams, attn_norm_g, q, k_new, v_new, cache_k, cache_v, zb, *, layer, lam_init):
    b, l, _ = q.shape
    rows = cache_k.shape[2]
    lam_spec = pl.BlockSpec((1, HEAD_DIM), lambda bi: (0, 0))
    new_spec = pl.BlockSpec((None, l, D_MODEL), lambda bi: (bi, 0, 0))
    cache_spec = pl.BlockSpec((None, None, rows, V_DIM), lambda bi: (layer, bi, 0, 0))
    return pl.pallas_call(
        functools.partial(_attn_sample_kernel, lam_init=lam_init),
        grid=(b,),
        in_specs=[lam_spec] * 4 + [
            pl.BlockSpec((1, V_DIM), lambda bi: (0, 0)),
            new_spec, new_spec, new_spec, cache_spec, cache_spec, new_spec],
        out_specs=new_spec,
        out_shape=jax.ShapeDtypeStruct((b, l, D_MODEL), BF16),
        compiler_params=pltpu.CompilerParams(
            dimension_semantics=("arbitrary",), vmem_limit_bytes=VMEM_LIMIT_BYTES),
        name="attn_sample",
    )(*lams, attn_norm_g, q, k_new, v_new, cache_k, cache_v, zb)


def _outproj_kernel(x_ref, yb_ref, ma_ref, gb_ref, wpb_ref, wout_ref, fg_ref, o_ref, *, final):
    p_b = jnp.dot(yb_ref[...], wpb_ref[...], preferred_element_type=F32)
    merged = ma_ref[...].astype(F32) + gb_ref[...].astype(F32) * p_b
    x_new = x_ref[...] + jnp.dot(merged.astype(BF16), wout_ref[...], preferred_element_type=F32)
    if final:
        x_new = _rms_scale(x_new) * fg_ref[...]
    o_ref[...] = x_new


def _outproj(x2d, yb, ma, gb, w_pb_bf, w_out_bf, final_g, *, tm, final):
    m = x2d.shape[0]
    row_spec = pl.BlockSpec((tm, D_MODEL), lambda i: (i, 0))
    w_spec = pl.BlockSpec((D_MODEL, D_MODEL), lambda i: (0, 0))
    return pl.pallas_call(
        functools.partial(_outproj_kernel, final=final),
        grid=(m // tm,),
        in_specs=[row_spec, row_spec, row_spec, row_spec, w_spec, w_spec,
                  pl.BlockSpec((1, D_MODEL), lambda i: (0, 0))],
        out_specs=row_spec,
        out_shape=jax.ShapeDtypeStruct((m, D_MODEL), F32),
        compiler_params=pltpu.CompilerParams(
            dimension_semantics=("arbitrary",), vmem_limit_bytes=VMEM_LIMIT_BYTES),
        name="outproj",
    )(x2d, yb, ma, gb, w_pb_bf, w_out_bf, final_g)


def kernel(x_prompt, x_sample, cache_k, cache_v, norm_g, w_in, w_s, b_s, v_norm_g,
           lam_q1, lam_k1, lam_q2, lam_k2, attn_norm_g, w_pa, w_pb, w_out, final_norm_g):
    depth = w_in.shape[0]
    bp, sp, _ = x_prompt.shape
    bs_, ls, _ = x_sample.shape
    mlp_chunk = w_s.shape[-1]
    past = cache_k.shape[2]
    cache_k4 = cache_k.reshape(depth, bs_, past * N_HEADS, V_DIM)
    cache_v4 = cache_v.reshape(depth, bs_, past * N_HEADS, V_DIM)
    xp = x_prompt.reshape(bp * sp, D_MODEL)
    xs = x_sample.reshape(bs_ * ls, D_MODEL)
    final_g = final_norm_g.reshape(1, D_MODEL)
    hist_p, hist_s = (), ()
    for li in range(depth):
        lam_init = 0.8 - 0.6 * math.exp(-0.3 * li)
        final = li == depth - 1
        w_in_bf = w_in[li].astype(BF16)
        w_pa_bf = w_pa[li].astype(BF16)
        w_pb_bf = w_pb[li].astype(BF16)
        w_out_bf = w_out[li].astype(BF16)
        ng = norm_g[li].reshape(1, D_MODEL)
        vg = v_norm_g[li].reshape(1, D_MODEL)
        ang = attn_norm_g[li].reshape(1, V_DIM)
        lams = tuple(a[li].reshape(1, HEAD_DIM) for a in (lam_q1, lam_k1, lam_q2, lam_k2))
        bs_full = jnp.repeat(jnp.transpose(b_s[li]), GROUP_W, axis=1)

        q, kb, vb, zb, ma, gb, *hist_p = _inproj(
            xp, ng, w_in_bf, w_s[li], bs_full, vg, w_pa_bf, hist_p,
            t_chunk=mlp_chunk, tm=256, emit_av=False, layer=li)
        to_seq = lambda a: a.reshape(bp, sp, D_MODEL)
        yb = _attn_prompt(lams, ang, to_seq(q), to_seq(kb), to_seq(vb), to_seq(zb),
                          lam_init=lam_init, tq=1024)
        xp = _outproj(xp, yb.reshape(bp * sp, D_MODEL), ma, gb, w_pb_bf, w_out_bf, final_g,
                      tm=512, final=final)

        q, kb, vb, zb, ma, gb, *hist_s = _inproj(
            xs, ng, w_in_bf, w_s[li][:, :ls, :ls], bs_full[:ls], vg, w_pa_bf, hist_s,
            t_chunk=ls, tm=256, emit_av=True, layer=li)
        to_seq = lambda a: a.reshape(bs_, ls, D_MODEL)
        yb = _attn_sample(lams, ang, to_seq(q), to_seq(kb), to_seq(vb), cache_k4, cache_v4,
                          to_seq(zb), layer=li, lam_init=lam_init)
        xs = _outproj(xs, yb.reshape(bs_ * ls, D_MODEL), ma, gb, w_pb_bf, w_out_bf, final_g,
                      tm=256, final=final)

    y_prompt = xp.reshape(bp, sp, D_MODEL)
    y_sample = xs.reshape(bs_, ls, D_MODEL)
    new_k_prompt = hist_p[0].reshape(depth, bp, sp, N_HEADS, V_DIM)
    new_v_prompt = hist_p[1].reshape(depth, bp, sp, N_HEADS, V_DIM)
    new_k_sample = hist_s[0].reshape(depth, bs_, ls, N_HEADS, V_DIM)
    new_v_sample = hist_s[1].reshape(depth, bs_, ls, N_HEADS, V_DIM)
    new_mlpv_sample = hist_s[2].reshape(depth, bs_, ls, D_MODEL)
    return (y_prompt, y_sample, new_k_prompt, new_v_prompt, new_k_sample, new_v_sample,
            new_mlpv_sample)
```

```python
import functools
import math

import jax
import jax.numpy as jnp
from jax import lax
from jax.experimental import pallas as pl
from jax.experimental.pallas import tpu as pltpu

F32 = jnp.float32
BF16 = jnp.bfloat16

D_MODEL = 1024
N_GROUPS = 4
GROUP_W = D_MODEL // N_GROUPS
N_HEADS = 8
HEAD_DIM = 64
V_DIM = 2 * HEAD_DIM
CHUNK = 64
EPS = 1e-6
N_SPLITS = 9
NEG_BIG = -1e30
Q_SCALE = HEAD_DIM ** -0.5 * math.log2(math.e)
SQRT_2_OVER_PI = math.sqrt(2.0 / math.pi)

VMEM_LIMIT_BYTES = 56 * 1024 * 1024


def _gelu_tanh(x):
    return x * (0.5 * (1.0 + jnp.tanh(SQRT_2_OVER_PI * (x + 0.044715 * (x * x * x)))))


def _sigmoid(x):
    return 1.0 / (1.0 + jnp.exp(-x))


def _silu(x):
    return x * _sigmoid(x)


def _rms_scale(x):
    return x * lax.rsqrt(jnp.mean(x * x, axis=-1, keepdims=True) + EPS)


def _store_heads(o_ref, val):
    tm = val.shape[0]
    for h in range(N_HEADS):
        o_ref[pl.ds(h, tm, stride=N_HEADS), :] = val[:, h * V_DIM:(h + 1) * V_DIM]


def _inproj_kernel(x_ref, ng_ref, win_ref, ws_ref, bs_ref, vg_ref, wpa_ref, *rest,
                   t_chunk, emit_av, layer):
    n_hist = 3 if emit_av else 2
    prev = rest[:n_hist] if layer else ()
    rest = rest[len(prev):]
    q_ref, kb_ref, vb_ref, zb_ref, ma_ref, gb_ref = rest[:6]
    hist = rest[6:6 + n_hist]
    vb_scr, sp_scr = rest[6 + n_hist:]
    for old, new in zip(prev, hist):
        new[:layer] = old[...]
    k_ref, v_ref = hist[0].at[layer], hist[1].at[layer]
    if emit_av:
        av_ref = hist[2].at[layer]
    tm = x_ref.shape[0]
    h = (_rms_scale(x_ref[...]) * ng_ref[...]).astype(BF16)

    def proj(j):
        return jnp.dot(h, win_ref[:, j * D_MODEL:(j + 1) * D_MODEL], preferred_element_type=F32)

    a_v = _rms_scale(_gelu_tanh(proj(1))) * vg_ref[...]
    if emit_av:
        av_ref[...] = a_v
    vb_scr[...] = a_v.astype(BF16)
    row = lax.broadcasted_iota(jnp.int32, (t_chunk, t_chunk), 0)
    col = lax.broadcasted_iota(jnp.int32, (t_chunk, t_chunk), 1)
    for g in range(N_GROUPS):
        w_g = jnp.where(col <= row, ws_ref[g], 0.0).astype(BF16)
        lanes = slice(g * GROUP_W, (g + 1) * GROUP_W)
        for c in range(tm // t_chunk):
            rows = slice(c * t_chunk, (c + 1) * t_chunk)
            sp_scr[rows, lanes] = (
                jnp.dot(w_g, vb_scr[rows, lanes], preferred_element_type=F32) + bs_ref[:, lanes])
    a_u = _gelu_tanh(proj(0))
    y_a = (_silu(proj(2)) * (a_u * sp_scr[...])).astype(BF16)
    p_a = jnp.dot(y_a, wpa_ref[...], preferred_element_type=F32)
    ma_ref[...] = (_sigmoid(proj(7)) * p_a).astype(BF16)
    q_ref[...] = (proj(3) * Q_SCALE).astype(BF16)
    b_k = proj(4)
    kb_ref[...] = b_k.astype(BF16)
    _store_heads(k_ref, b_k)
    b_v = proj(5)
    vb_ref[...] = b_v.astype(BF16)
    _store_heads(v_ref, b_v)
    zb_ref[...] = _silu(proj(6)).astype(BF16)
    gb_ref[...] = _sigmoid(proj(8)).astype(BF16)


def _inproj(x2d, norm_g, w_in_bf, w_s, bs_full, v_norm_g, w_pa_bf, prev, *, t_chunk, tm, emit_av,
            layer):
    m = x2d.shape[0]
    row_spec = pl.BlockSpec((tm, D_MODEL), lambda i: (i, 0))
    full = lambda shape: pl.BlockSpec(shape, lambda i: (0,) * len(shape))
    of_layer = lambda shape: pl.BlockSpec((None,) + shape, lambda i: (layer,) + (0,) * len(shape))
    hist_tiles = [(tm * N_HEADS, V_DIM), (tm * N_HEADS, V_DIM)] + ([(tm, D_MODEL)] if emit_av else [])
    hist_rows = [m * N_HEADS, m * N_HEADS] + ([m] if emit_av else [])
    hist_spec = lambda n, tile: pl.BlockSpec((n,) + tile, lambda i: (0, i, 0))
    bf16_out = jax.ShapeDtypeStruct((m, D_MODEL), BF16)
    out_shape = [bf16_out] * 6 + [
        jax.ShapeDtypeStruct((layer + 1, rows, tile[1]), F32)
        for rows, tile in zip(hist_rows, hist_tiles)]
    out_specs = [row_spec] * 6 + [hist_spec(layer + 1, tile) for tile in hist_tiles]
    prev_specs = [hist_spec(layer, tile) for tile in hist_tiles] if layer else []
    return pl.pallas_call(
        functools.partial(_inproj_kernel, t_chunk=t_chunk, emit_av=emit_av, layer=layer),
        grid=(m // tm,),
        in_specs=[
            row_spec,
            full((1, D_MODEL)),
            of_layer((D_MODEL, N_SPLITS * D_MODEL)),
            full((N_GROUPS, t_chunk, t_chunk)),
            full((t_chunk, D_MODEL)),
            full((1, D_MODEL)),
            of_layer((D_MODEL, D_MODEL)),
        ] + prev_specs,
        out_specs=out_specs,
        out_shape=out_shape,
        scratch_shapes=[pltpu.VMEM((tm, D_MODEL), BF16), pltpu.VMEM((tm, D_MODEL), F32)],
        compiler_params=pltpu.CompilerParams(
            dimension_semantics=("arbitrary",), vmem_limit_bytes=VMEM_LIMIT_BYTES),
        name="inproj",
    )(x2d, norm_g, w_in_bf, w_s, bs_full, v_norm_g, w_pa_bf, *prev)


def _lambda(lq1_ref, lk1_ref, lq2_ref, lk2_ref, lam_init):
    s1 = jnp.sum(lq1_ref[...] * lk1_ref[...], axis=-1, keepdims=True)
    s2 = jnp.sum(lq2_ref[...] * lk2_ref[...], axis=-1, keepdims=True)
    return jnp.exp(s1) - jnp.exp(s2) + lam_init


KEY_HALF = 256
HEADS_PER_STEP = 1


def _attn_prompt_kernel(lq1_ref, lk1_ref, lq2_ref, lk2_ref, ang_ref, q_ref, k_ref, v_ref, zb_ref,
                        o_ref, kb_scr, vt_scr, qz_scr, s_scr, e_scr, mj_scr, acc_scr,
                        *, tq, lam_init):
    qi = pl.program_id(2)
    n_halves = kb_scr.shape[1]
    n_sub = tq // KEY_HALF
    chains = [(hh, a) for hh in range(HEADS_PER_STEP) for a in range(2)]

    def scores(hh, a, j, lanes=slice(None)):
        return jnp.dot(kb_scr[hh, j], qz_scr[hh, a, :, lanes], preferred_element_type=F32)

    def prepare(q_block):
        rows = pl.ds(pl.multiple_of(q_block * tq, tq), tq)
        comp = lax.broadcasted_iota(jnp.int32, (V_DIM, tq), 0)
        for hh in range(HEADS_PER_STEP):
            q_t = q_ref[rows, hh * V_DIM:(hh + 1) * V_DIM].astype(F32).T
            qz_scr[hh, 0] = jnp.where(comp < HEAD_DIM, q_t, 0.0).astype(BF16)
            qz_scr[hh, 1] = jnp.where(comp >= HEAD_DIM, q_t, 0.0).astype(BF16)
        for hh, a in chains:
            s_scr[hh, a] = scores(hh, a, 0)

    @pl.when(qi == 0)
    def _():
        for hh in range(HEADS_PER_STEP):
            lanes = slice(hh * V_DIM, (hh + 1) * V_DIM)
            for c in range(n_halves):
                rows = slice(c * KEY_HALF, (c + 1) * KEY_HALF)
                kb_scr[hh, c] = k_ref[rows, lanes]
                vt_scr[hh, c] = v_ref[rows, lanes].astype(F32).T.astype(BF16)
        prepare(0)

    lam = _lambda(lq1_ref, lk1_ref, lq2_ref, lk2_ref, lam_init)
    acc_scr[...] = jnp.zeros_like(acc_scr)

    def stats(s_t, hh, a, j, m_old, l_old, lanes=slice(None)):
        m_new = jnp.maximum(m_old, jnp.max(s_t, axis=0, keepdims=True))
        e_t = jnp.exp2(s_t - m_new)
        l_new = jnp.exp2(m_old - m_new) * l_old + jnp.sum(e_t, axis=0, keepdims=True)
        e_scr[hh, a, j, :, lanes] = e_t.astype(BF16)
        mj_scr[hh, a, j, :, lanes] = m_new
        return m_new, l_new

    def tile_lanes(d):
        return slice(d * KEY_HALF, tq)

    def group_stats(t, st):
        out = []
        for i, (hh, a) in enumerate(chains):
            m, l = st[2 * i], st[2 * i + 1]
            later = [scores(hh, a, n_sub * t + d) for d in range(1, n_sub)]
            m, l = stats(s_scr[hh, a], hh, a, n_sub * t, m, l)
            s_scr[hh, a] = scores(hh, a, n_sub * (t + 1))
            for d, s_d in enumerate(later, start=1):
                m, l = stats(s_d, hh, a, n_sub * t + d, m, l)
            out += [m, l]
        return tuple(out)

    init = (jnp.full((1, tq), NEG_BIG, F32), jnp.zeros((1, tq), F32)) * len(chains)
    st = lax.fori_loop(0, qi, group_stats, init)
    k_chunk = lax.broadcasted_iota(jnp.int32, (KEY_HALF, KEY_HALF), 0) // CHUNK
    q_chunk = lax.broadcasted_iota(jnp.int32, (KEY_HALF, KEY_HALF), 1) // CHUNK
    triangle = k_chunk <= q_chunk

    def diag_scores(s_t):
        head = jnp.where(triangle, s_t[:, :KEY_HALF], NEG_BIG)
        return head if s_t.shape[1] == KEY_HALF else jnp.concatenate(
            [head, s_t[:, KEY_HALF:]], axis=1)

    m_fin, w_fin = {}, {}
    for i, (hh, a) in enumerate(chains):
        m, l = st[2 * i], st[2 * i + 1]
        later = [scores(hh, a, n_sub * qi + d, tile_lanes(d)) for d in range(1, n_sub)]
        m, l = stats(diag_scores(s_scr[hh, a]), hh, a, n_sub * qi, m, l)
        for d, s_d in enumerate(later, start=1):
            lanes = tile_lanes(d)
            m_d, l_d = stats(diag_scores(s_d), hh, a, n_sub * qi + d, m[:, lanes], l[:, lanes],
                             lanes)
            m = jnp.concatenate([m[:, :d * KEY_HALF], m_d], axis=1)
            l = jnp.concatenate([l[:, :d * KEY_HALF], l_d], axis=1)
        m_fin[hh, a] = m
        w_fin[hh, a] = (1.0 if a == 0 else lam) / l

    def weights_pv(hh, j, lanes=slice(None)):
        c = [(jnp.exp2(mj_scr[hh, a, j, :, lanes] - m_fin[hh, a][:, lanes])
              * w_fin[hh, a][:, lanes]).astype(BF16) for a in range(2)]
        a_t = e_scr[hh, 0, j, :, lanes] * c[0] - e_scr[hh, 1, j, :, lanes] * c[1]
        acc_scr[hh, :, lanes] += jnp.dot(vt_scr[hh, j], a_t, preferred_element_type=F32)

    def group_pv(t, carry):
        for hh in range(HEADS_PER_STEP):
            for d in range(n_sub):
                weights_pv(hh, n_sub * t + d)
        return carry

    lax.fori_loop(0, qi, group_pv, 0)
    gain = ang_ref[...] * (1.0 - lam_init)
    for hh in range(HEADS_PER_STEP):
        lanes = slice(hh * V_DIM, (hh + 1) * V_DIM)
        for d in range(n_sub):
            weights_pv(hh, n_sub * qi + d, tile_lanes(d))
        o_t = acc_scr[hh]
        o_t = o_t * lax.rsqrt(jnp.mean(o_t * o_t, axis=0, keepdims=True) + EPS)
        o_ref[:, lanes] = (zb_ref[:, lanes].astype(F32) * (o_t.T * gain)).astype(BF16)
    prepare(jnp.minimum(qi + 1, pl.num_programs(2) - 1))


def _attn_prompt(lams, attn_norm_g, q, k, v, zb, *, lam_init, tq):
    b, s, _ = q.shape
    n_halves = s // KEY_HALF
    width = HEADS_PER_STEP * V_DIM
    lam_spec = pl.BlockSpec((1, HEAD_DIM), lambda bi, hi, qi: (0, 0))
    head_rows = pl.BlockSpec((None, tq, width), lambda bi, hi, qi: (bi, qi, hi))
    head_all = pl.BlockSpec((None, s, width), lambda bi, hi, qi: (bi, 0, hi))
    return pl.pallas_call(
        functools.partial(_attn_prompt_kernel, tq=tq, lam_init=lam_init),
        grid=(b, N_HEADS // HEADS_PER_STEP, s // tq),
        in_specs=[lam_spec] * 4 + [
            pl.BlockSpec((1, V_DIM), lambda bi, hi, qi: (0, 0)),
            head_all, head_all, head_all, head_rows],
        out_specs=head_rows,
        out_shape=jax.ShapeDtypeStruct((b, s, D_MODEL), BF16),
        scratch_shapes=[
            pltpu.VMEM((HEADS_PER_STEP, n_halves, KEY_HALF, V_DIM), BF16),
            pltpu.VMEM((HEADS_PER_STEP, n_halves, V_DIM, KEY_HALF), BF16),
            pltpu.VMEM((HEADS_PER_STEP, 2, V_DIM, tq), BF16),
            pltpu.VMEM((HEADS_PER_STEP, 2, KEY_HALF, tq), F32),
            pltpu.VMEM((HEADS_PER_STEP, 2, n_halves, KEY_HALF, tq), BF16),
            pltpu.VMEM((HEADS_PER_STEP, 2, n_halves, 1, tq), F32),
            pltpu.VMEM((HEADS_PER_STEP, V_DIM, tq), F32),
        ],
        compiler_params=pltpu.CompilerParams(
            dimension_semantics=("arbitrary", "arbitrary", "arbitrary"),
            vmem_limit_bytes=VMEM_LIMIT_BYTES),
        name="attn_prompt",
    )(*lams, attn_norm_g, q, k, v, zb)


def _attn_sample_kernel(lq1_ref, lk1_ref, lq2_ref, lk2_ref, ang_ref, q_ref, kn_ref, vn_ref,
                        kc_ref, vc_ref, zb_ref, o_ref, *, lam_init):
    lam = _lambda(lq1_ref, lk1_ref, lq2_ref, lk2_ref, lam_init)
    past = kc_ref.shape[0] // N_HEADS
    n_new = q_ref.shape[0]
    gain = ang_ref[...] * (1.0 - lam_init)
    comp = lax.broadcasted_iota(jnp.int32, (n_new, V_DIM), 1)
    contract_last = (((1,), (1,)), ((), ()))
    for h in range(N_HEADS):
        lanes = slice(h * V_DIM, (h + 1) * V_DIM)
        q = q_ref[:, lanes]
        zero = jnp.zeros_like(q)
        qz = jnp.concatenate([jnp.where(comp < HEAD_DIM, q, zero),
                              jnp.where(comp >= HEAD_DIM, q, zero)], axis=0)
        k_c = kc_ref[pl.ds(h, past, stride=N_HEADS), :].astype(BF16)
        v_c = vc_ref[pl.ds(h, past, stride=N_HEADS), :].astype(BF16)
        s_c = lax.dot_general(qz, k_c, contract_last, preferred_element_type=F32)
        s_n = lax.dot_general(qz, kn_ref[:, lanes], contract_last, preferred_element_type=F32)
        m = jnp.maximum(jnp.max(s_c, axis=-1, keepdims=True), jnp.max(s_n, axis=-1, keepdims=True))
        p_c = jnp.exp2(s_c - m)
        p_n = jnp.exp2(s_n - m)
        l = jnp.sum(p_c, axis=-1, keepdims=True) + jnp.sum(p_n, axis=-1, keepdims=True)
        pv = (jnp.dot(p_c.astype(BF16), v_c, preferred_element_type=F32)
              + jnp.dot(p_n.astype(BF16), vn_ref[:, lanes], preferred_element_type=F32)) / l
        o = pv[:n_new] - lam * pv[n_new:]
        o_ref[:, lanes] = (zb_ref[:, lanes].astype(F32) * (_rms_scale(o) * gain)).astype(BF16)


def _attn_sample(lams, attn_norm_g, q, k_new, v_new, cache_k, cache_v, zb, *, layer, lam_init):
    b, l, _ = q.shape
    rows = cache_k.shape[2]
    lam_spec = pl.BlockSpec((1, HEAD_DIM), lambda bi: (0, 0))
    new_spec = pl.BlockSpec((None, l, D_MODEL), lambda bi: (bi, 0, 0))
    cache_spec = pl.BlockSpec((None, None, rows, V_DIM), lambda bi: (layer, bi, 0, 0))
    return pl.pallas_call(
        functools.partial(_attn_sample_kernel, lam_init=lam_init),
        grid=(b,),
        in_specs=[lam_spec] * 4 + [
            pl.BlockSpec((1, V_DIM), lambda bi: (0, 0)),
            new_spec, new_spec, new_spec, cache_spec, cache_spec, new_spec],
        out_specs=new_spec,
        out_shape=jax.ShapeDtypeStruct((b, l, D_MODEL), BF16),
        compiler_params=pltpu.CompilerParams(
            dimension_semantics=("arbitrary",), vmem_limit_bytes=VMEM_LIMIT_BYTES),
        name="attn_sample",
    )(*lams, attn_norm_g, q, k_new, v_new, cache_k, cache_v, zb)


def _outproj_kernel(x_ref, yb_ref, ma_ref, gb_ref, wpb_ref, wout_ref, fg_ref, o_ref, *, final):
    p_b = jnp.dot(yb_ref[...], wpb_ref[...], preferred_element_type=F32)
    merged = ma_ref[...].astype(F32) + gb_ref[...].astype(F32) * p_b
    x_new = x_ref[...] + jnp.dot(merged.astype(BF16), wout_ref[...], preferred_element_type=F32)
    if final:
        x_new = _rms_scale(x_new) * fg_ref[...]
    o_ref[...] = x_new


def _outproj(x2d, yb, ma, gb, w_pb_bf, w_out_bf, final_g, *, tm, final, layer):
    m = x2d.shape[0]
    row_spec = pl.BlockSpec((tm, D_MODEL), lambda i: (i, 0))
    w_spec = pl.BlockSpec((None, D_MODEL, D_MODEL), lambda i: (layer, 0, 0))
    return pl.pallas_call(
        functools.partial(_outproj_kernel, final=final),
        grid=(m // tm,),
        in_specs=[row_spec, row_spec, row_spec, row_spec, w_spec, w_spec,
                  pl.BlockSpec((1, D_MODEL), lambda i: (0, 0))],
        out_specs=row_spec,
        out_shape=jax.ShapeDtypeStruct((m, D_MODEL), F32),
        compiler_params=pltpu.CompilerParams(
            dimension_semantics=("arbitrary",), vmem_limit_bytes=VMEM_LIMIT_BYTES),
        name="outproj",
    )(x2d, yb, ma, gb, w_pb_bf, w_out_bf, final_g)


def kernel(x_prompt, x_sample, cache_k, cache_v, norm_g, w_in, w_s, b_s, v_norm_g,
           lam_q1, lam_k1, lam_q2, lam_k2, attn_norm_g, w_pa, w_pb, w_out, final_norm_g):
    depth = w_in.shape[0]
    bp, sp, _ = x_prompt.shape
    bs_, ls, _ = x_sample.shape
    mlp_chunk = w_s.shape[-1]
    past = cache_k.shape[2]
    cache_k4 = cache_k.reshape(depth, bs_, past * N_HEADS, V_DIM)
    cache_v4 = cache_v.reshape(depth, bs_, past * N_HEADS, V_DIM)
    xp = x_prompt.reshape(bp * sp, D_MODEL)
    xs = x_sample.reshape(bs_ * ls, D_MODEL)
    final_g = final_norm_g.reshape(1, D_MODEL)
    w_in_bf, w_pa_bf, w_pb_bf, w_out_bf = (w.astype(BF16) for w in (w_in, w_pa, w_pb, w_out))
    hist_p, hist_s = (), ()
    for li in range(depth):
        lam_init = 0.8 - 0.6 * math.exp(-0.3 * li)
        final = li == depth - 1
        ng = norm_g[li].reshape(1, D_MODEL)
        vg = v_norm_g[li].reshape(1, D_MODEL)
        ang = attn_norm_g[li].reshape(1, V_DIM)
        lams = tuple(a[li].reshape(1, HEAD_DIM) for a in (lam_q1, lam_k1, lam_q2, lam_k2))
        bs_full = jnp.repeat(jnp.transpose(b_s[li]), GROUP_W, axis=1)

        q, kb, vb, zb, ma, gb, *hist_p = _inproj(
            xp, ng, w_in_bf, w_s[li], bs_full, vg, w_pa_bf, hist_p,
            t_chunk=mlp_chunk, tm=256, emit_av=False, layer=li)
        to_seq = lambda a: a.reshape(bp, sp, D_MODEL)
        yb = _attn_prompt(lams, ang, to_seq(q), to_seq(kb), to_seq(vb), to_seq(zb),
                          lam_init=lam_init, tq=1024)
        xp = _outproj(xp, yb.reshape(bp * sp, D_MODEL), ma, gb, w_pb_bf, w_out_bf, final_g,
                      tm=512, final=final, layer=li)

        q, kb, vb, zb, ma, gb, *hist_s = _inproj(
            xs, ng, w_in_bf, w_s[li][:, :ls, :ls], bs_full[:ls], vg, w_pa_bf, hist_s,
            t_chunk=ls, tm=256, emit_av=True, layer=li)
        to_seq = lambda a: a.reshape(bs_, ls, D_MODEL)
        yb = _attn_sample(lams, ang, to_seq(q), to_seq(kb), to_seq(vb), cache_k4, cache_v4,
                          to_seq(zb), layer=li, lam_init=lam_init)
        xs = _outproj(xs, yb.reshape(bs_ * ls, D_MODEL), ma, gb, w_pb_bf, w_out_bf, final_g,
                      tm=256, final=final, layer=li)

    y_prompt = xp.reshape(bp, sp, D_MODEL)
    y_sample = xs.reshape(bs_, ls, D_MODEL)
    new_k_prompt = hist_p[0].reshape(depth, bp, sp, N_HEADS, V_DIM)
    new_v_prompt = hist_p[1].reshape(depth, bp, sp, N_HEADS, V_DIM)
    new_k_sample = hist_s[0].reshape(depth, bs_, ls, N_HEADS, V_DIM)
    new_v_sample = hist_s[1].reshape(depth, bs_, ls, N_HEADS, V_DIM)
    new_mlpv_sample = hist_s[2].reshape(depth, bs_, ls, D_MODEL)
    return (y_prompt, y_sample, new_k_prompt, new_v_prompt, new_k_sample, new_v_sample,
            new_mlpv_sample)
```

```python
import functools
import math

import jax
import jax.numpy as jnp
from jax import lax
from jax.experimental import pallas as pl
from jax.experimental.pallas import tpu as pltpu

F32 = jnp.float32
BF16 = jnp.bfloat16

D_MODEL = 1024
N_GROUPS = 4
GROUP_W = D_MODEL // N_GROUPS
N_HEADS = 8
HEAD_DIM = 64
V_DIM = 2 * HEAD_DIM
CHUNK = 64
EPS = 1e-6
N_SPLITS = 9
NEG_BIG = -1e30
Q_SCALE = HEAD_DIM ** -0.5 * math.log2(math.e)
SQRT_2_OVER_PI = math.sqrt(2.0 / math.pi)

VMEM_LIMIT_BYTES = 56 * 1024 * 1024
ATTN_VMEM_LIMIT_BYTES = 62 * 1024 * 1024


def _gelu_tanh(x):
    return x * (0.5 * (1.0 + jnp.tanh(SQRT_2_OVER_PI * (x + 0.044715 * (x * x * x)))))


def _sigmoid(x):
    return 1.0 / (1.0 + jnp.exp(-x))


def _silu(x):
    return x * _sigmoid(x)


def _rms_scale(x):
    return x * lax.rsqrt(jnp.mean(x * x, axis=-1, keepdims=True) + EPS)


def _store_heads(o_ref, val):
    tm = val.shape[0]
    for h in range(N_HEADS):
        o_ref[pl.ds(h, tm, stride=N_HEADS), :] = val[:, h * V_DIM:(h + 1) * V_DIM]


def _inproj_kernel(x_ref, ng_ref, win_ref, ws_ref, bs_ref, vg_ref, wpa_ref, *rest,
                   t_chunk, emit_av, layer):
    n_hist = 3 if emit_av else 2
    prev = rest[:n_hist] if layer else ()
    rest = rest[len(prev):]
    q_ref, kb_ref, vb_ref, zb_ref, ma_ref, gb_ref = rest[:6]
    hist = rest[6:6 + n_hist]
    vb_scr, sp_scr = rest[6 + n_hist:]
    for old, new in zip(prev, hist):
        new[:layer] = old[...]
    k_ref, v_ref = hist[0].at[layer], hist[1].at[layer]
    if emit_av:
        av_ref = hist[2].at[layer]
    tm = x_ref.shape[0]
    h = (_rms_scale(x_ref[...]) * ng_ref[...]).astype(BF16)

    def proj(j):
        return jnp.dot(h, win_ref[:, j * D_MODEL:(j + 1) * D_MODEL], preferred_element_type=F32)

    a_v = _rms_scale(_gelu_tanh(proj(1))) * vg_ref[...]
    if emit_av:
        av_ref[...] = a_v
    vb_scr[...] = a_v.astype(BF16)
    row = lax.broadcasted_iota(jnp.int32, (t_chunk, t_chunk), 0)
    col = lax.broadcasted_iota(jnp.int32, (t_chunk, t_chunk), 1)
    for g in range(N_GROUPS):
        w_g = jnp.where(col <= row, ws_ref[g], 0.0).astype(BF16)
        lanes = slice(g * GROUP_W, (g + 1) * GROUP_W)
        for c in range(tm // t_chunk):
            rows = slice(c * t_chunk, (c + 1) * t_chunk)
            sp_scr[rows, lanes] = (
                jnp.dot(w_g, vb_scr[rows, lanes], preferred_element_type=F32) + bs_ref[:, lanes])
    a_u = _gelu_tanh(proj(0))
    y_a = (_silu(proj(2)) * (a_u * sp_scr[...])).astype(BF16)
    p_a = jnp.dot(y_a, wpa_ref[...], preferred_element_type=F32)
    ma_ref[...] = (_sigmoid(proj(7)) * p_a).astype(BF16)
    q_ref[...] = (proj(3) * Q_SCALE).astype(BF16)
    b_k = proj(4)
    kb_ref[...] = b_k.astype(BF16)
    _store_heads(k_ref, b_k)
    b_v = proj(5)
    vb_ref[...] = b_v.astype(BF16)
    _store_heads(v_ref, b_v)
    zb_ref[...] = _silu(proj(6)).astype(BF16)
    gb_ref[...] = _sigmoid(proj(8)).astype(BF16)


def _inproj(x2d, norm_g, w_in_bf, w_s, bs_full, v_norm_g, w_pa_bf, prev, *, t_chunk, tm, emit_av,
            layer):
    m = x2d.shape[0]
    row_spec = pl.BlockSpec((tm, D_MODEL), lambda i: (i, 0))
    full = lambda shape: pl.BlockSpec(shape, lambda i: (0,) * len(shape))
    of_layer = lambda shape: pl.BlockSpec((None,) + shape, lambda i: (layer,) + (0,) * len(shape))
    hist_tiles = [(tm * N_HEADS, V_DIM), (tm * N_HEADS, V_DIM)] + ([(tm, D_MODEL)] if emit_av else [])
    hist_rows = [m * N_HEADS, m * N_HEADS] + ([m] if emit_av else [])
    hist_spec = lambda n, tile: pl.BlockSpec((n,) + tile, lambda i: (0, i, 0))
    bf16_out = jax.ShapeDtypeStruct((m, D_MODEL), BF16)
    out_shape = [bf16_out] * 6 + [
        jax.ShapeDtypeStruct((layer + 1, rows, tile[1]), F32)
        for rows, tile in zip(hist_rows, hist_tiles)]
    out_specs = [row_spec] * 6 + [hist_spec(layer + 1, tile) for tile in hist_tiles]
    prev_specs = [hist_spec(layer, tile) for tile in hist_tiles] if layer else []
    return pl.pallas_call(
        functools.partial(_inproj_kernel, t_chunk=t_chunk, emit_av=emit_av, layer=layer),
        grid=(m // tm,),
        in_specs=[
            row_spec,
            full((1, D_MODEL)),
            of_layer((D_MODEL, N_SPLITS * D_MODEL)),
            full((N_GROUPS, t_chunk, t_chunk)),
            full((t_chunk, D_MODEL)),
            full((1, D_MODEL)),
            of_layer((D_MODEL, D_MODEL)),
        ] + prev_specs,
        out_specs=out_specs,
        out_shape=out_shape,
        scratch_shapes=[pltpu.VMEM((tm, D_MODEL), BF16), pltpu.VMEM((tm, D_MODEL), F32)],
        compiler_params=pltpu.CompilerParams(
            dimension_semantics=("arbitrary",), vmem_limit_bytes=VMEM_LIMIT_BYTES),
        name="inproj",
    )(x2d, norm_g, w_in_bf, w_s, bs_full, v_norm_g, w_pa_bf, *prev)


def _lambda(lq1_ref, lk1_ref, lq2_ref, lk2_ref, lam_init):
    s1 = jnp.sum(lq1_ref[...] * lk1_ref[...], axis=-1, keepdims=True)
    s2 = jnp.sum(lq2_ref[...] * lk2_ref[...], axis=-1, keepdims=True)
    return jnp.exp(s1) - jnp.exp(s2) + lam_init


KEY_HALF = 256
HEADS_PER_STEP = 2


def _attn_prompt_kernel(lq1_ref, lk1_ref, lq2_ref, lk2_ref, ang_ref, q_ref, k_ref, v_ref, zb_ref,
                        o_ref, kb_scr, vt_scr, qz_scr, s_scr, e_scr, mj_scr, acc_scr,
                        *, tq, lam_init):
    qi = pl.program_id(2)
    n_halves = kb_scr.shape[1]
    n_sub = tq // KEY_HALF
    chains = [(hh, a) for hh in range(HEADS_PER_STEP) for a in range(2)]

    def scores(hh, a, j, lanes=slice(None)):
        return jnp.dot(kb_scr[hh, j], qz_scr[hh, a, :, lanes], preferred_element_type=F32)

    def prepare(q_block):
        rows = pl.ds(pl.multiple_of(q_block * tq, tq), tq)
        comp = lax.broadcasted_iota(jnp.int32, (V_DIM, tq), 0)
        for hh in range(HEADS_PER_STEP):
            q_t = q_ref[rows, hh * V_DIM:(hh + 1) * V_DIM].astype(F32).T
            qz_scr[hh, 0] = jnp.where(comp < HEAD_DIM, q_t, 0.0).astype(BF16)
            qz_scr[hh, 1] = jnp.where(comp >= HEAD_DIM, q_t, 0.0).astype(BF16)
        for hh, a in chains:
            s_scr[hh, a] = scores(hh, a, 0)

    @pl.when(qi == 0)
    def _():
        for hh in range(HEADS_PER_STEP):
            lanes = slice(hh * V_DIM, (hh + 1) * V_DIM)
            for c in range(n_halves):
                rows = slice(c * KEY_HALF, (c + 1) * KEY_HALF)
                kb_scr[hh, c] = k_ref[rows, lanes]
                vt_scr[hh, c] = v_ref[rows, lanes].astype(F32).T.astype(BF16)
        prepare(0)

    lam = _lambda(lq1_ref, lk1_ref, lq2_ref, lk2_ref, lam_init)
    acc_scr[...] = jnp.zeros_like(acc_scr)

    def stats(s_t, hh, a, j, m_old, l_old, lanes=slice(None)):
        m_new = jnp.maximum(m_old, jnp.max(s_t, axis=0, keepdims=True))
        e_t = jnp.exp2(s_t - m_new)
        l_new = jnp.exp2(m_old - m_new) * l_old + jnp.sum(e_t, axis=0, keepdims=True)
        e_scr[hh, a, j, :, lanes] = e_t.astype(BF16)
        mj_scr[hh, a, j, :, lanes] = m_new
        return m_new, l_new

    def tile_lanes(d):
        return slice(d * KEY_HALF, tq)

    def group_stats(t, st):
        out = []
        for i, (hh, a) in enumerate(chains):
            m, l = st[2 * i], st[2 * i + 1]
            later = [scores(hh, a, n_sub * t + d) for d in range(1, n_sub)]
            m, l = stats(s_scr[hh, a], hh, a, n_sub * t, m, l)
            s_scr[hh, a] = scores(hh, a, n_sub * (t + 1))
            for d, s_d in enumerate(later, start=1):
                m, l = stats(s_d, hh, a, n_sub * t + d, m, l)
            out += [m, l]
        return tuple(out)

    init = (jnp.full((1, tq), NEG_BIG, F32), jnp.zeros((1, tq), F32)) * len(chains)
    st = lax.fori_loop(0, qi, group_stats, init)
    k_chunk = lax.broadcasted_iota(jnp.int32, (KEY_HALF, KEY_HALF), 0) // CHUNK
    q_chunk = lax.broadcasted_iota(jnp.int32, (KEY_HALF, KEY_HALF), 1) // CHUNK
    triangle = k_chunk <= q_chunk

    def diag_scores(s_t):
        head = jnp.where(triangle, s_t[:, :KEY_HALF], NEG_BIG)
        return head if s_t.shape[1] == KEY_HALF else jnp.concatenate(
            [head, s_t[:, KEY_HALF:]], axis=1)

    m_fin, w_fin = {}, {}
    for i, (hh, a) in enumerate(chains):
        m, l = st[2 * i], st[2 * i + 1]
        later = [scores(hh, a, n_sub * qi + d, tile_lanes(d)) for d in range(1, n_sub)]
        m, l = stats(diag_scores(s_scr[hh, a]), hh, a, n_sub * qi, m, l)
        for d, s_d in enumerate(later, start=1):
            lanes = tile_lanes(d)
            m_d, l_d = stats(diag_scores(s_d), hh, a, n_sub * qi + d, m[:, lanes], l[:, lanes],
                             lanes)
            m = jnp.concatenate([m[:, :d * KEY_HALF], m_d], axis=1)
            l = jnp.concatenate([l[:, :d * KEY_HALF], l_d], axis=1)
        m_fin[hh, a] = m
        w_fin[hh, a] = (1.0 if a == 0 else lam) / l

    def weights_pv(hh, j, lanes=slice(None)):
        c = [(jnp.exp2(mj_scr[hh, a, j, :, lanes] - m_fin[hh, a][:, lanes])
              * w_fin[hh, a][:, lanes]).astype(BF16) for a in range(2)]
        a_t = e_scr[hh, 0, j, :, lanes] * c[0] - e_scr[hh, 1, j, :, lanes] * c[1]
        acc_scr[hh, :, lanes] += jnp.dot(vt_scr[hh, j], a_t, preferred_element_type=F32)

    def group_pv(t, carry):
        for hh in range(HEADS_PER_STEP):
            for d in range(n_sub):
                weights_pv(hh, n_sub * t + d)
        return carry

    lax.fori_loop(0, qi, group_pv, 0)
    gain = ang_ref[...] * (1.0 - lam_init)
    for hh in range(HEADS_PER_STEP):
        lanes = slice(hh * V_DIM, (hh + 1) * V_DIM)
        for d in range(n_sub):
            weights_pv(hh, n_sub * qi + d, tile_lanes(d))
        o_t = acc_scr[hh]
        o_t = o_t * lax.rsqrt(jnp.mean(o_t * o_t, axis=0, keepdims=True) + EPS)
        o_ref[:, lanes] = (zb_ref[:, lanes].astype(F32) * (o_t.T * gain)).astype(BF16)
    prepare(jnp.minimum(qi + 1, pl.num_programs(2) - 1))


def _attn_prompt(lams, attn_norm_g, q, k, v, zb, *, lam_init, tq):
    b, s, _ = q.shape
    n_halves = s // KEY_HALF
    width = HEADS_PER_STEP * V_DIM
    lam_spec = pl.BlockSpec((1, HEAD_DIM), lambda bi, hi, qi: (0, 0))
    head_rows = pl.BlockSpec((None, tq, width), lambda bi, hi, qi: (bi, qi, hi))
    head_all = pl.BlockSpec((None, s, width), lambda bi, hi, qi: (bi, 0, hi))
    return pl.pallas_call(
        functools.partial(_attn_prompt_kernel, tq=tq, lam_init=lam_init),
        grid=(b, N_HEADS // HEADS_PER_STEP, s // tq),
        in_specs=[lam_spec] * 4 + [
            pl.BlockSpec((1, V_DIM), lambda bi, hi, qi: (0, 0)),
            head_all, head_all, head_all, head_rows],
        out_specs=head_rows,
        out_shape=jax.ShapeDtypeStruct((b, s, D_MODEL), BF16),
        scratch_shapes=[
            pltpu.VMEM((HEADS_PER_STEP, n_halves, KEY_HALF, V_DIM), BF16),
            pltpu.VMEM((HEADS_PER_STEP, n_halves, V_DIM, KEY_HALF), BF16),
            pltpu.VMEM((HEADS_PER_STEP, 2, V_DIM, tq), BF16),
            pltpu.VMEM((HEADS_PER_STEP, 2, KEY_HALF, tq), F32),
            pltpu.VMEM((HEADS_PER_STEP, 2, n_halves, KEY_HALF, tq), BF16),
            pltpu.VMEM((HEADS_PER_STEP, 2, n_halves, 1, tq), F32),
            pltpu.VMEM((HEADS_PER_STEP, V_DIM, tq), F32),
        ],
        compiler_params=pltpu.CompilerParams(
            dimension_semantics=("arbitrary", "arbitrary", "arbitrary"),
            vmem_limit_bytes=ATTN_VMEM_LIMIT_BYTES),
        name="attn_prompt",
    )(*lams, attn_norm_g, q, k, v, zb)


def _attn_sample_kernel(lq1_ref, lk1_ref, lq2_ref, lk2_ref, ang_ref, q_ref, kn_ref, vn_ref,
                        kc_ref, vc_ref, zb_ref, o_ref, *, lam_init):
    lam = _lambda(lq1_ref, lk1_ref, lq2_ref, lk2_ref, lam_init)
    past = kc_ref.shape[0] // N_HEADS
    n_new = q_ref.shape[0]
    gain = ang_ref[...] * (1.0 - lam_init)
    comp = lax.broadcasted_iota(jnp.int32, (n_new, V_DIM), 1)
    contract_last = (((1,), (1,)), ((), ()))
    for h in range(N_HEADS):
        lanes = slice(h * V_DIM, (h + 1) * V_DIM)
        q = q_ref[:, lanes]
        zero = jnp.zeros_like(q)
        qz = jnp.concatenate([jnp.where(comp < HEAD_DIM, q, zero),
                              jnp.where(comp >= HEAD_DIM, q, zero)], axis=0)
        k_c = kc_ref[pl.ds(h, past, stride=N_HEADS), :].astype(BF16)
        v_c = vc_ref[pl.ds(h, past, stride=N_HEADS), :].astype(BF16)
        s_c = lax.dot_general(qz, k_c, contract_last, preferred_element_type=F32)
        s_n = lax.dot_general(qz, kn_ref[:, lanes], contract_last, preferred_element_type=F32)
        m = jnp.maximum(jnp.max(s_c, axis=-1, keepdims=True), jnp.max(s_n, axis=-1, keepdims=True))
        p_c = jnp.exp2(s_c - m)
        p_n = jnp.exp2(s_n - m)
        l = jnp.sum(p_c, axis=-1, keepdims=True) + jnp.sum(p_n, axis=-1, keepdims=True)
        pv = (jnp.dot(p_c.astype(BF16), v_c, preferred_element_type=F32)
              + jnp.dot(p_n.astype(BF16), vn_ref[:, lanes], preferred_element_type=F32)) / l
        o = pv[:n_new] - lam * pv[n_new:]
        o_ref[:, lanes] = (zb_ref[:, lanes].astype(F32) * (_rms_scale(o) * gain)).astype(BF16)


def _attn_sample(lams, attn_norm_g, q, k_new, v_new, cache_k, cache_v, zb, *, layer, lam_init):
    b, l, _ = q.shape
    rows = cache_k.shape[2]
    lam_spec = pl.BlockSpec((1, HEAD_DIM), lambda bi: (0, 0))
    new_spec = pl.BlockSpec((None, l, D_MODEL), lambda bi: (bi, 0, 0))
    cache_spec = pl.BlockSpec((None, None, rows, V_DIM), lambda bi: (layer, bi, 0, 0))
    return pl.pallas_call(
        functools.partial(_attn_sample_kernel, lam_init=lam_init),
        grid=(b,),
        in_specs=[lam_spec] * 4 + [
            pl.BlockSpec((1, V_DIM), lambda bi: (0, 0)),
            new_spec, new_spec, new_spec, cache_spec, cache_spec, new_spec],
        out_specs=new_spec,
        out_shape=jax.ShapeDtypeStruct((b, l, D_MODEL), BF16),
        compiler_params=pltpu.CompilerParams(
            dimension_semantics=("arbitrary",), vmem_limit_bytes=VMEM_LIMIT_BYTES),
        name="attn_sample",
    )(*lams, attn_norm_g, q, k_new, v_new, cache_k, cache_v, zb)


def _outproj_kernel(x_ref, yb_ref, ma_ref, gb_ref, wpb_ref, wout_ref, fg_ref, o_ref, *, final):
    p_b = jnp.dot(yb_ref[...], wpb_ref[...], preferred_element_type=F32)
    merged = ma_ref[...].astype(F32) + gb_ref[...].astype(F32) * p_b
    x_new = x_ref[...] + jnp.dot(merged.astype(BF16), wout_ref[...], preferred_element_type=F32)
    if final:
        x_new = _rms_scale(x_new) * fg_ref[...]
    o_ref[...] = x_new


def _outproj(x2d, yb, ma, gb, w_pb_bf, w_out_bf, final_g, *, tm, final, layer):
    m = x2d.shape[0]
    row_spec = pl.BlockSpec((tm, D_MODEL), lambda i: (i, 0))
    w_spec = pl.BlockSpec((None, D_MODEL, D_MODEL), lambda i: (layer, 0, 0))
    return pl.pallas_call(
        functools.partial(_outproj_kernel, final=final),
        grid=(m // tm,),
        in_specs=[row_spec, row_spec, row_spec, row_spec, w_spec, w_spec,
                  pl.BlockSpec((1, D_MODEL), lambda i: (0, 0))],
        out_specs=row_spec,
        out_shape=jax.ShapeDtypeStruct((m, D_MODEL), F32),
        compiler_params=pltpu.CompilerParams(
            dimension_semantics=("arbitrary",), vmem_limit_bytes=VMEM_LIMIT_BYTES),
        name="outproj",
    )(x2d, yb, ma, gb, w_pb_bf, w_out_bf, final_g)


def kernel(x_prompt, x_sample, cache_k, cache_v, norm_g, w_in, w_s, b_s, v_norm_g,
           lam_q1, lam_k1, lam_q2, lam_k2, attn_norm_g, w_pa, w_pb, w_out, final_norm_g):
    depth = w_in.shape[0]
    bp, sp, _ = x_prompt.shape
    bs_, ls, _ = x_sample.shape
    mlp_chunk = w_s.shape[-1]
    past = cache_k.shape[2]
    cache_k4 = cache_k.reshape(depth, bs_, past * N_HEADS, V_DIM)
    cache_v4 = cache_v.reshape(depth, bs_, past * N_HEADS, V_DIM)
    xp = x_prompt.reshape(bp * sp, D_MODEL)
    xs = x_sample.reshape(bs_ * ls, D_MODEL)
    final_g = final_norm_g.reshape(1, D_MODEL)
    w_in_bf, w_pa_bf, w_pb_bf, w_out_bf = (w.astype(BF16) for w in (w_in, w_pa, w_pb, w_out))
    hist_p, hist_s = (), ()
    for li in range(depth):
        lam_init = 0.8 - 0.6 * math.exp(-0.3 * li)
        final = li == depth - 1
        ng = norm_g[li].reshape(1, D_MODEL)
        vg = v_norm_g[li].reshape(1, D_MODEL)
        ang = attn_norm_g[li].reshape(1, V_DIM)
        lams = tuple(a[li].reshape(1, HEAD_DIM) for a in (lam_q1, lam_k1, lam_q2, lam_k2))
        bs_full = jnp.repeat(jnp.transpose(b_s[li]), GROUP_W, axis=1)

        q, kb, vb, zb, ma, gb, *hist_p = _inproj(
            xp, ng, w_in_bf, w_s[li], bs_full, vg, w_pa_bf, hist_p,
            t_chunk=mlp_chunk, tm=256, emit_av=False, layer=li)
        to_seq = lambda a: a.reshape(bp, sp, D_MODEL)
        yb = _attn_prompt(lams, ang, to_seq(q), to_seq(kb), to_seq(vb), to_seq(zb),
                          lam_init=lam_init, tq=1024)
        xp = _outproj(xp, yb.reshape(bp * sp, D_MODEL), ma, gb, w_pb_bf, w_out_bf, final_g,
                      tm=512, final=final, layer=li)

        q, kb, vb, zb, ma, gb, *hist_s = _inproj(
            xs, ng, w_in_bf, w_s[li][:, :ls, :ls], bs_full[:ls], vg, w_pa_bf, hist_s,
            t_chunk=ls, tm=256, emit_av=True, layer=li)
        to_seq = lambda a: a.reshape(bs_, ls, D_MODEL)
        yb = _attn_sample(lams, ang, to_seq(q), to_seq(kb), to_seq(vb), cache_k4, cache_v4,
                          to_seq(zb), layer=li, lam_init=lam_init)
        xs = _outproj(xs, yb.reshape(bs_ * ls, D_MODEL), ma, gb, w_pb_bf, w_out_bf, final_g,
                      tm=256, final=final, layer=li)

    y_prompt = xp.reshape(bp, sp, D_MODEL)
    y_sample = xs.reshape(bs_, ls, D_MODEL)
    new_k_prompt = hist_p[0].reshape(depth, bp, sp, N_HEADS, V_DIM)
    new_v_prompt = hist_p[1].reshape(depth, bp, sp, N_HEADS, V_DIM)
    new_k_sample = hist_s[0].reshape(depth, bs_, ls, N_HEADS, V_DIM)
    new_v_sample = hist_s[1].reshape(depth, bs_, ls, N_HEADS, V_DIM)
    new_mlpv_sample = hist_s[2].reshape(depth, bs_, ls, D_MODEL)
    return (y_prompt, y_sample, new_k_prompt, new_v_prompt, new_k_sample, new_v_sample,
            new_mlpv_sample)
```
